```python
import functools
import jax, jax.numpy as jnp
from jax import lax
import numpy as np

D_MODEL = 1024
BATCH = 8
SEQ = 4096
DEPTH = 2
DEC_BATCH = 32
DEC_SEQ = 8
PAST_LEN = 16384
PAGE_SIZE = 128

N_EVEN = (DEPTH + 1) // 2
N_ODD = DEPTH // 2
EPS = 1e-6
CHUNK = 128
A_W = D_MODEL // 2
A_G = 4
A_C = A_W // A_G
B_W = D_MODEL // 2
HB = 8
NB = B_W // HB
W_LORA = 64
A_LORA = 64
G_LORA = 128
P_B = 3 * B_W + W_LORA + A_LORA + G_LORA
GN_EPS = 64e-5
C_W = D_MODEL // 2
NBLK = 8
BW = C_W // NBLK
CONV_W = 4
LRU_C = 8.0
H_D = 8
HD = 64
N_KV = 2
H_I = 8
D_I = 32
TOPK_MAX = 256
Q_BLOCK = 128
P_D = H_D * HD + 2 * N_KV * HD + H_I * D_I + D_I + H_I
ROPE_THETA = 500000.0
P_EVEN = 2 * A_W + P_B
P_ODD = 2 * C_W + P_D
MEM = 256
XH = 4
XHD = D_MODEL // XH
D_FF = ((8 * D_MODEL // 3 + 255) // 256) * 256

kernel_name = 'hybrid_gmlp_rwkv7_rglru_dsa_decode_step'


def rms_norm(x, g):
    xf = x.astype(jnp.float32)
    y = xf * lax.rsqrt(jnp.mean(xf * xf, -1, keepdims=True) + EPS)
    return (y * g.astype(jnp.float32)).astype(x.dtype)


def layer_norm(x, g, b):
    xf = x.astype(jnp.float32)
    mu = jnp.mean(xf, -1, keepdims=True)
    var = jnp.mean(jnp.square(xf - mu), -1, keepdims=True)
    return ((xf - mu) * lax.rsqrt(var + EPS) * g + b).astype(x.dtype)


def split_cols(p, sizes):
    return jnp.split(p, np.cumsum(sizes)[:-1].tolist(), axis=-1)


def gather_rows(a, idx):
    return jax.vmap(lambda ab, ib: ab[ib])(a, idx)


def rope(x, pos):
    rot = x.shape[-1] // 4
    half = rot // 2
    inv = ROPE_THETA ** (-jnp.arange(half, dtype=jnp.float32) * 2.0 / rot)
    ang = pos.astype(jnp.float32)[:, None] * inv[None, :]
    cos = jnp.cos(ang)[:, None, :]
    sin = jnp.sin(ang)[:, None, :]
    xf = x.astype(jnp.float32)
    x1, x2 = xf[..., :half], xf[..., half:rot]
    out = jnp.concatenate([x1 * cos - x2 * sin, x2 * cos + x1 * sin, xf[..., rot:]], -1)
    return out.astype(x.dtype)


def chunk_mixer(pa, ln_g, ln_b, ws, bs):
    z = jax.nn.gelu(pa)
    u, v = z[..., :A_W], z[..., A_W:]
    v = layer_norm(v, ln_g, ln_b)
    B, S, _ = v.shape
    n = -(-S // CHUNK)
    vp = jnp.pad(v, ((0, 0), (0, n * CHUNK - S), (0, 0))).reshape(B, n, CHUNK, A_G, A_C)
    s = jnp.einsum('gts,bnsgc->bntgc', jnp.tril(ws), vp) + bs.T[:, :, None]
    s = s.reshape(B, n * CHUNK, A_W)[:, :S]
    return u * s, v


def rwkv_scan(S0, r, w, k, v, kk, a):
    def step(St, inp):
        r_t, w_t, k_t, v_t, kk_t, a_t = inp
        sa = jnp.einsum('bhij,bhj->bhi', St, kk_t)
        St = St * w_t[:, :, None, :] - sa[..., None] * (kk_t * a_t)[:, :, None, :] + v_t[..., None] * k_t[:, :, None, :]
        return St, jnp.einsum('bhij,bhj->bhi', St, r_t)
    xs = tuple(jnp.moveaxis(t, 1, 0) for t in (r, w, k, v, kk, a))
    S_last, ys = lax.scan(step, S0, xs)
    return jnp.moveaxis(ys, 0, 1), S_last


def rwkv_mixer(pb, S0, shift0, mu, w0, wB, a0, aB, gB, k_k, k_a, r_k, gn_g, gn_b):
    B, S, _ = pb.shape
    prev = jnp.concatenate([shift0[:, None].astype(pb.dtype), pb[:, :-1]], 1)
    pm = (pb + (prev - pb) * mu).astype(jnp.float32)
    r, w_lo, k, v, a_lo, g_lo = split_cols(pm, [B_W, W_LORA, B_W, B_W, A_LORA, G_LORA])
    w = -jax.nn.softplus(-(w0 + jnp.tanh(w_lo) @ wB)) - 0.5
    decay = jnp.exp(-jnp.exp(w))
    a = jax.nn.sigmoid(a0 + a_lo @ aB)
    g = jax.nn.sigmoid(g_lo) @ gB
    hs = lambda t: t.reshape(B, S, HB, NB)
    kk = hs(k * k_k)
    kk = kk * lax.rsqrt(jnp.sum(kk * kk, -1, keepdims=True) + 1e-12)
    k = k * (1.0 + (a - 1.0) * k_a)
    r, k, v, a, decay = hs(r), hs(k), hs(v), hs(a), hs(decay)
    y, S_last = rwkv_scan(S0.astype(jnp.float32), r, decay, k, v, kk, a)
    mean = jnp.mean(y, -1, keepdims=True)
    var = jnp.mean(jnp.square(y - mean), -1, keepdims=True)
    y = ((y - mean) * lax.rsqrt(var + GN_EPS)).reshape(B, S, B_W) * gn_g + gn_b
    y = y + (jnp.sum(r * k * r_k, -1, keepdims=True) * v).reshape(B, S, B_W)
    y = y * g
    return y.astype(pb.dtype), S_last.astype(pb.dtype), pb[:, -1]


def causal_conv(x, buf, w, b):
    S = x.shape[1]
    xc = jnp.concatenate([buf.astype(x.dtype), x], 1)
    y = b + xc[:, 0:S] * w[0]
    for j in range(1, CONV_W):
        y = y + xc[:, j:j + S] * w[j]
    return y, xc[:, S:]


def lru_combine(e1, e2):
    a1, b1 = e1
    a2, b2 = e2
    return a1 * a2, a2 * b1 + b2


def rg_lru(x, h0, reset, wx, bx, wa, ba, lam):
    B, S, _ = x.shape
    xb = x.reshape(B, S, NBLK, BW)
    gx = jax.nn.sigmoid(jnp.einsum('bsnc,ncd->bsnd', xb, wx).reshape(B, S, C_W) + bx)
    ga = jax.nn.sigmoid(jnp.einsum('bsnc,ncd->bsnd', xb, wa).reshape(B, S, C_W) + ba)
    log_a = LRU_C * ga * jax.nn.log_sigmoid(lam)
    a = jnp.exp(log_a)
    mult = jnp.sqrt(-jnp.expm1(2.0 * log_a))
    mult = jnp.where(reset[None, :, None], 1.0, mult)
    bterm = x * gx * mult
    bterm = bterm.at[:, 0].add(a[:, 0] * h0)
    _, h = lax.associative_scan(lru_combine, (a, bterm), axis=1)
    return h, h[:, -1]


def lru_mixer(pc, h0, conv0, reset, conv_w, conv_b, wx, bx, wa, ba, lam):
    gate_pre, xin = split_cols(pc, [C_W, C_W])
    xconv, conv_new = causal_conv(xin, conv0, conv_w, conv_b)
    h, h_last = rg_lru(xconv.astype(jnp.float32), h0.astype(jnp.float32), reset, wx, bx, wa, ba, lam)
    y = h * jax.nn.gelu(gate_pre.astype(jnp.float32))
    return y.astype(pc.dtype), h_last.astype(pc.dtype), conv_new


def indexer_scores(qi, wi, ki):
    s = jax.nn.relu(jnp.einsum('bthd,bld->bthl', qi, ki).astype(jnp.float32))
    return jnp.einsum('bthl,bth->btl', s, wi.astype(jnp.float32) * (H_I ** -0.5 * D_I ** -0.5))


def gqa_attend(q, kg, vg, valid):
    B, T = q.shape[0], q.shape[1]
    qg = q.reshape(B, T, N_KV, H_D // N_KV, HD)
    logits = jnp.einsum('btkgd,btnkd->btkgn', qg, kg).astype(jnp.float32) * (HD ** -0.5)
    logits = jnp.where(valid[:, :, None, None, :], logits, -jnp.inf)
    p = jax.nn.softmax(logits, -1).astype(vg.dtype)
    return jnp.einsum('btkgn,btnkd->btkgd', p, vg).reshape(B, T, H_D * HD)


def dsa_prompt(q, k, v, qi, ki, wi):
    B, S = q.shape[0], q.shape[1]
    topk = min(TOPK_MAX, S // 4)
    key_pos = jnp.arange(S)

    def block(i):
        t0 = i * Q_BLOCK
        sl = lambda t: lax.dynamic_slice_in_dim(t, t0, Q_BLOCK, axis=1)
        qpos = t0 + jnp.arange(Q_BLOCK)
        score = indexer_scores(sl(qi), sl(wi), ki)
        score = jnp.where((key_pos[None, :] <= qpos[:, None])[None], score, -jnp.inf)
        _, idx = lax.top_k(score, topk)
        valid = idx <= qpos[None, :, None]
        return gqa_attend(sl(q), gather_rows(k, idx), gather_rows(v, idx), valid)

    out = lax.map(block, jnp.arange(S // Q_BLOCK))
    return jnp.moveaxis(out, 0, 1).reshape(B, S, H_D * HD)


def dsa_sample(q, k_new, v_new, qi, ki_new, wi, pool_k, pool_v, pool_ki, page_table):
    B, T = q.shape[0], q.shape[1]
    past = page_table.shape[1] * PAGE_SIZE
    L = past + T
    topk = min(TOPK_MAX, L // 4)
    ki_past = pool_ki[page_table].reshape(B, past, D_I)
    ki_all = jnp.concatenate([ki_past.astype(ki_new.dtype), ki_new], 1)
    qpos = past + jnp.arange(T)
    score = indexer_scores(qi, wi, ki_all)
    score = jnp.where(jnp.arange(L)[None, None, :] <= qpos[None, :, None], score, -jnp.inf)
    _, idx = lax.top_k(score, topk)
    valid = idx <= qpos[None, :, None]
    in_past = (idx < past)[..., None, None]
    pidx = jnp.minimum(idx, past - 1)
    phys = jnp.take_along_axis(page_table, (pidx // PAGE_SIZE).reshape(B, -1), axis=1).reshape(idx.shape)
    off = pidx % PAGE_SIZE
    nidx = jnp.clip(idx - past, 0, T - 1)
    kg = jnp.where(in_past, pool_k[phys, off].astype(k_new.dtype), gather_rows(k_new, nidx))
    vg = jnp.where(in_past, pool_v[phys, off].astype(v_new.dtype), gather_rows(v_new, nidx))
    return gqa_attend(q, kg, vg, valid)


def dsa_mixer(pd, pos, attend):
    B, S, _ = pd.shape
    q, k, v, qi, ki, wi = split_cols(pd, [H_D * HD, N_KV * HD, N_KV * HD, H_I * D_I, D_I, H_I])
    q = rope(q.reshape(B, S, H_D, HD), pos)
    k = rope(k.reshape(B, S, N_KV, HD), pos)
    v = v.reshape(B, S, N_KV, HD)
    qi = rope(qi.reshape(B, S, H_I, D_I), pos)
    ki = rope(ki[:, :, None, :], pos)[:, :, 0]
    return attend(q, k, v, qi, ki, wi), k, v, ki


def cross_attend(h, mk, mv, wq, wo):
    B, S, _ = h.shape
    q = (h @ wq).reshape(B, S, XH, XHD)
    logits = jnp.einsum('bshd,bmhd->bhsm', q, mk).astype(jnp.float32) * (XHD ** -0.5)
    p = jax.nn.softmax(logits, -1).astype(mv.dtype)
    return jnp.einsum('bhsm,bmhd->bshd', p, mv).reshape(B, S, XH * XHD) @ wo


def swiglu(h, wg, wu, wd):
    return (jax.nn.silu(h @ wg) * (h @ wu)) @ wd


def memory_kv(mem, norm_mem, w_xk, w_xv):
    B, M, _ = mem.shape
    ks, vs = [], []
    for layer in range(DEPTH):
        m = rms_norm(mem, norm_mem[layer])
        ks.append((m @ w_xk[layer]).reshape(B, M, XH, XHD))
        vs.append((m @ w_xv[layer]).reshape(B, M, XH, XHD))
    return jnp.stack(ks, 0), jnp.stack(vs, 0)


def trunk(x, pos0, rwkv_s0, rwkv_shift0, lru_h0, lru_conv0, mem_k, mem_v, attend, P):
    B, S, _ = x.shape
    pos = pos0 + jnp.arange(S, dtype=jnp.int32)
    reset = pos == 0
    chunk_v, rwkv_s, rwkv_sh, lru_h, lru_cv, att_k, att_v, att_ki = [], [], [], [], [], [], [], []
    for layer in range(DEPTH):
        j = layer // 2
        h = rms_norm(x, P['norm_mix'][layer])
        if layer % 2 == 0:
            pa, pb = split_cols(h @ P['e_w_in'][j], [2 * A_W, P_B])
            ya, v_rows = chunk_mixer(pa, P['a_ln_g'][j], P['a_ln_b'][j], P['a_ws'][j], P['a_bs'][j])
            yb, s_new, sh_new = rwkv_mixer(pb, rwkv_s0[j], rwkv_shift0[j], P['b_mu'][j], P['b_w0'][j], P['b_wB'][j],
                                           P['b_a0'][j], P['b_aB'][j], P['b_gB'][j], P['b_kk'][j], P['b_ka'][j],
                                           P['b_rk'][j], P['b_gn_g'][j], P['b_gn_b'][j])
            mix = jnp.concatenate([ya, yb.astype(ya.dtype)], -1) @ P['e_w_out'][j]
            chunk_v.append(v_rows)
            rwkv_s.append(s_new)
            rwkv_sh.append(sh_new)
        else:
            pc, pd = split_cols(h @ P['o_w_in'][j], [2 * C_W, P_D])
            yc, h_last, conv_new = lru_mixer(pc, lru_h0[j], lru_conv0[j], reset, P['c_conv_w'][j], P['c_conv_b'][j],
                                             P['c_wx'][j], P['c_bx'][j], P['c_wa'][j], P['c_ba'][j], P['c_lambda'][j])
            yd, k, v, ki = dsa_mixer(pd, pos, functools.partial(attend, j))
            mix = jnp.concatenate([yc, yd.astype(yc.dtype)], -1) @ P['o_w_out'][j]
            lru_h.append(h_last)
            lru_cv.append(conv_new)
            att_k.append(k)
            att_v.append(v)
            att_ki.append(ki)
        x = x + mix
        h = rms_norm(x, P['norm_x'][layer])
        x = x + cross_attend(h, mem_k[layer], mem_v[layer], P['w_xq'][layer], P['w_xo'][layer])
        h = rms_norm(x, P['norm_ffn'][layer])
        x = x + swiglu(h, P['w_ff_gate'][layer], P['w_ff_up'][layer], P['w_ff_down'][layer])
    y = rms_norm(x, P['norm_final'])
    st = lambda l: jnp.stack(l, 0)
    return y, st(chunk_v), st(rwkv_s), st(rwkv_sh), st(lru_h), st(lru_cv), st(att_k), st(att_v), st(att_ki)


def setup_inputs(seed: int = 0) -> dict:
    key = jax.random.key(seed)
    ks = iter(jax.random.split(key, 80))
    f32 = jnp.float32

    def nrm(shape, scale=1.0):
        return jax.random.normal(next(ks), shape, f32) * scale

    def gain(shape):
        return 1.0 + 0.02 * jax.random.normal(next(ks), shape, f32)

    n_pages = PAST_LEN // PAGE_SIZE
    n_used = DEC_BATCH * n_pages
    n_pool = n_used + max(1, n_used // 4)
    page_table = jax.random.permutation(next(ks), n_pool)[:n_used].reshape(DEC_BATCH, n_pages).astype(jnp.int32)
    lam_p = jax.random.uniform(next(ks), (N_ODD, C_W), f32, 0.9, 0.999)
    return {
        'x_prompt': nrm((BATCH, SEQ, D_MODEL)),
        'x_sample': nrm((DEC_BATCH, DEC_SEQ, D_MODEL)),
        'mem_prompt': nrm((BATCH, MEM, D_MODEL)),
        'state_rwkv': nrm((N_EVEN, DEC_BATCH, HB, NB, NB), 0.5),
        'state_rwkv_shift': nrm((N_EVEN, DEC_BATCH, P_B)),
        'state_lru_h': nrm((N_ODD, DEC_BATCH, C_W), 0.5),
        'state_lru_conv': nrm((N_ODD, DEC_BATCH, CONV_W - 1, C_W)),
        'cache_attn_k': nrm((N_ODD, n_pool, PAGE_SIZE, N_KV, HD)),
        'cache_attn_v': nrm((N_ODD, n_pool, PAGE_SIZE, N_KV, HD)),
        'cache_attn_kidx': nrm((N_ODD, n_pool, PAGE_SIZE, D_I)),
        'cache_mem_k': nrm((DEPTH, DEC_BATCH, MEM, XH, XHD)),
        'cache_mem_v': nrm((DEPTH, DEC_BATCH, MEM, XH, XHD)),
        'page_table': page_table,
        'norm_mix': gain((DEPTH, D_MODEL)),
        'norm_x': gain((DEPTH, D_MODEL)),
        'norm_mem': gain((DEPTH, D_MODEL)),
        'norm_ffn': gain((DEPTH, D_MODEL)),
        'norm_final': gain((D_MODEL,)),
        'w_xq': nrm((DEPTH, D_MODEL, XH * XHD), D_MODEL ** -0.5),
        'w_xk': nrm((DEPTH, D_MODEL, XH * XHD), D_MODEL ** -0.5),
        'w_xv': nrm((DEPTH, D_MODEL, XH * XHD), D_MODEL ** -0.5),
        'w_xo': nrm((DEPTH, XH * XHD, D_MODEL), (XH * XHD) ** -0.5),
        'w_ff_gate': nrm((DEPTH, D_MODEL, D_FF), D_MODEL ** -0.5),
        'w_ff_up': nrm((DEPTH, D_MODEL, D_FF), D_MODEL ** -0.5),
        'w_ff_down': nrm((DEPTH, D_FF, D_MODEL), D_FF ** -0.5),
        'e_w_in': nrm((N_EVEN, D_MODEL, P_EVEN), D_MODEL ** -0.5),
        'e_w_out': nrm((N_EVEN, A_W + B_W, D_MODEL), (A_W + B_W) ** -0.5),
        'a_ln_g': gain((N_EVEN, A_W)),
        'a_ln_b': nrm((N_EVEN, A_W), 0.02),
        'a_ws': nrm((N_EVEN, A_G, CHUNK, CHUNK), CHUNK ** -0.5),
        'a_bs': gain((N_EVEN, A_G, CHUNK)),
        'b_mu': jax.random.uniform(next(ks), (N_EVEN, P_B), f32),
        'b_w0': jax.random.uniform(next(ks), (N_EVEN, B_W), f32, -5.0, -1.0),
        'b_wB': nrm((N_EVEN, W_LORA, B_W), 0.1),
        'b_a0': nrm((N_EVEN, B_W), 0.1),
        'b_aB': nrm((N_EVEN, A_LORA, B_W), 0.1),
        'b_gB': nrm((N_EVEN, G_LORA, B_W), G_LORA ** -0.5),
        'b_kk': 1.0 + nrm((N_EVEN, B_W), 0.1),
        'b_ka': 1.0 + nrm((N_EVEN, B_W), 0.1),
        'b_rk': nrm((N_EVEN, HB, NB), 0.1),
        'b_gn_g': gain((N_EVEN, B_W)),
        'b_gn_b': nrm((N_EVEN, B_W), 0.02),
        'o_w_in': nrm((N_ODD, D_MODEL, P_ODD), D_MODEL ** -0.5),
        'o_w_out': nrm((N_ODD, C_W + H_D * HD, D_MODEL), (C_W + H_D * HD) ** -0.5),
        'c_conv_w': nrm((N_ODD, CONV_W, C_W), CONV_W ** -0.5),
        'c_conv_b': nrm((N_ODD, C_W), 0.02),
        'c_wx': nrm((N_ODD, NBLK, BW, BW), BW ** -0.5),
        'c_bx': nrm((N_ODD, C_W), 0.02),
        'c_wa': nrm((N_ODD, NBLK, BW, BW), BW ** -0.5),
        'c_ba': nrm((N_ODD, C_W), 0.02),
        'c_lambda': jnp.log(lam_p) - jnp.log1p(-lam_p),
    }


def reference(x_prompt, x_sample, mem_prompt, state_rwkv, state_rwkv_shift, state_lru_h, state_lru_conv,
              cache_attn_k, cache_attn_v, cache_attn_kidx, cache_mem_k, cache_mem_v, page_table,
              norm_mix, norm_x, norm_mem, norm_ffn, norm_final,
              w_xq, w_xk, w_xv, w_xo, w_ff_gate, w_ff_up, w_ff_down,
              e_w_in, e_w_out, a_ln_g, a_ln_b, a_ws, a_bs,
              b_mu, b_w0, b_wB, b_a0, b_aB, b_gB, b_kk, b_ka, b_rk, b_gn_g, b_gn_b,
              o_w_in, o_w_out, c_conv_w, c_conv_b, c_wx, c_bx, c_wa, c_ba, c_lambda):
    P = dict(norm_mix=norm_mix, norm_x=norm_x, norm_ffn=norm_ffn, norm_final=norm_final,
             w_xq=w_xq, w_xo=w_xo, w_ff_gate=w_ff_gate, w_ff_up=w_ff_up, w_ff_down=w_ff_down,
             e_w_in=e_w_in, e_w_out=e_w_out, a_ln_g=a_ln_g, a_ln_b=a_ln_b, a_ws=a_ws, a_bs=a_bs,
             b_mu=b_mu, b_w0=b_w0, b_wB=b_wB, b_a0=b_a0, b_aB=b_aB, b_gB=b_gB, b_kk=b_kk, b_ka=b_ka,
             b_rk=b_rk, b_gn_g=b_gn_g, b_gn_b=b_gn_b,
             o_w_in=o_w_in, o_w_out=o_w_out, c_conv_w=c_conv_w, c_conv_b=c_conv_b,
             c_wx=c_wx, c_bx=c_bx, c_wa=c_wa, c_ba=c_ba, c_lambda=c_lambda)
    dt = x_prompt.dtype
    bp = x_prompt.shape[0]

    p_mem_k, p_mem_v = memory_kv(mem_prompt, norm_mem, w_xk, w_xv)

    def attend_prompt(j, q, k, v, qi, ki, wi):
        return dsa_prompt(q, k, v, qi, ki, wi)

    (y_prompt, _, p_rwkv_state, p_rwkv_shift, p_lru_h, p_lru_conv,
     p_attn_k, p_attn_v, p_attn_kidx) = trunk(
        x_prompt, 0,
        jnp.zeros((N_EVEN, bp, HB, NB, NB), dt), jnp.zeros((N_EVEN, bp, P_B), dt),
        jnp.zeros((N_ODD, bp, C_W), dt), jnp.zeros((N_ODD, bp, CONV_W - 1, C_W), dt),
        p_mem_k, p_mem_v, attend_prompt, P)

    def attend_sample(j, q, k, v, qi, ki, wi):
        return dsa_sample(q, k, v, qi, ki, wi, cache_attn_k[j], cache_attn_v[j], cache_attn_kidx[j], page_table)

    (y_sample, s_chunk_v, s_rwkv_state, s_rwkv_shift, s_lru_h, s_lru_conv,
     s_attn_k, s_attn_v, s_attn_kidx) = trunk(
        x_sample, PAST_LEN, state_rwkv, state_rwkv_shift, state_lru_h, state_lru_conv,
        cache_mem_k, cache_mem_v, attend_sample, P)

    return (y_prompt, y_sample,
            p_rwkv_state, p_rwkv_shift, p_lru_h, p_lru_conv, p_attn_k, p_attn_v, p_attn_kidx, p_mem_k, p_mem_v,
            s_chunk_v, s_rwkv_state, s_rwkv_shift, s_lru_h, s_lru_conv, s_attn_k, s_attn_v, s_attn_kidx)
```

```python
import functools

import numpy as np
import jax
import jax.numpy as jnp
from jax import lax
from jax.experimental import pallas as pl
from jax.experimental.pallas import tpu as pltpu

F32 = jnp.float32
BF16 = jnp.bfloat16
I32 = jnp.int32
INT_MIN = -2 ** 31

EPS = 1e-6
GN_EPS = 64e-5
CHUNK = 128
A_G = 4
HB = 8
NB = 64
W_LORA = 64
A_LORA = 64
G_LORA = 128
NBLK = 8
CONV_W = 4
LRU_C = 8.0
H_D = 8
HD = 64
N_KV = 2
H_I = 8
D_I = 32
TOPK_MAX = 256
Q_BLOCK = 128
PAGE_SIZE = 128
ROPE_THETA = 500000.0
XH = 4
LANES = 128
SUBLANES = 8
VMEM_LIMIT = 56 * 1024 * 1024


def _params(*sem):
    return pltpu.CompilerParams(dimension_semantics=sem, vmem_limit_bytes=VMEM_LIMIT)


def _dot(a, b):
    return jnp.dot(a, b, preferred_element_type=F32)


def _dot_nt(a, b):
    return lax.dot_general(a, b, (((1,), (1,)), ((), ())), preferred_element_type=F32)


def _rms(x, g):
    return x * lax.rsqrt(jnp.mean(x * x, axis=-1, keepdims=True) + EPS) * g


def _segsum(x, ones_bf):
    hi = x.astype(BF16)
    lo = (x - hi.astype(F32)).astype(BF16)
    return _dot(hi, ones_bf) + _dot(lo, ones_bf)


def _softplus(x):
    return jnp.maximum(x, 0.0) + jnp.log1p(jnp.exp(-jnp.abs(x)))


def _full(shape):
    n = len(shape)
    return pl.BlockSpec(shape, lambda *_: (0,) * n)


def _rms_matmul_body(x_ref, g_ref, w_ref, *out_refs, splits):
    h = _rms(x_ref[...], g_ref[...]).astype(BF16)
    off = 0
    for o_ref, n in zip(out_refs, splits):
        o_ref[...] = _dot(h, w_ref[:, off:off + n])
        off += n


def rms_matmul(x, g, w_bf, splits):
    T, D = x.shape
    N = w_bf.shape[1]
    assert sum(splits) == N
    tm = min(256, T)
    return pl.pallas_call(
        functools.partial(_rms_matmul_body, splits=tuple(splits)),
        grid=(T // tm,),
        in_specs=[pl.BlockSpec((tm, D), lambda i: (i, 0)), _full((1, D)), _full((D, N))],
        out_specs=[pl.BlockSpec((tm, n), lambda i: (i, 0)) for n in splits],
        out_shape=[jax.ShapeDtypeStruct((T, n), F32) for n in splits],
        compiler_params=_params("arbitrary"), name="rms_matmul")(x, g.reshape(1, D), w_bf)


def _chunk_mixer_body(pa_ref, lng_ref, lnb_ref, m_ref, bias_ref, ya_ref, v_ref, *, ng):
    z = jax.nn.gelu(pa_ref[...])
    aw = z.shape[1] // 2
    u = z[:, :aw]
    v = z[:, aw:]
    mu = jnp.mean(v, -1, keepdims=True)
    var = jnp.mean(jnp.square(v - mu), -1, keepdims=True)
    v = (v - mu) * lax.rsqrt(var + EPS) * lng_ref[...] + lnb_ref[...]
    v_ref[...] = v
    vb = v.astype(BF16)
    ac = aw // ng
    for g in range(ng):
        s = _dot(m_ref[g], vb[:, g * ac:(g + 1) * ac]) + bias_ref[g]
        ya_ref[:, g * ac:(g + 1) * ac] = u[:, g * ac:(g + 1) * ac] * s


def chunk_mixer(pa, ln_g, ln_b, mix_bf, bias):
    T, W = pa.shape
    aw = W // 2
    ng, tm, _ = mix_bf.shape
    return pl.pallas_call(
        functools.partial(_chunk_mixer_body, ng=ng),
        grid=(T // tm,),
        in_specs=[pl.BlockSpec((tm, W), lambda i: (i, 0)), _full((1, aw)), _full((1, aw)),
                  _full(mix_bf.shape), _full(bias.shape)],
        out_specs=[pl.BlockSpec((tm, aw), lambda i: (i, 0))] * 2,
        out_shape=[jax.ShapeDtypeStruct((T, aw), F32)] * 2,
        compiler_params=_params("arbitrary"), name="chunk_mixer")(pa, ln_g.reshape(1, aw), ln_b.reshape(1, aw), mix_bf, bias)


def _rwkv_prep_body(pb_ref, sh0_ref, mu_ref, w0_ref, wB_ref, a0_ref, aB_ref, gB_ref, kk_ref, ka_ref, rk_ref, ones_ref,
                    r_out, w_out, k_out, v_out, kk_out, b_out, g_out, bv_out, last_out, carry_ref):
    j = pl.program_id(1)
    pb = pb_ref[0]
    tm = pb.shape[0]
    bw = r_out.shape[2]

    @pl.when(j == 0)
    def _():
        carry_ref[...] = sh0_ref[0]

    row = lax.broadcasted_iota(I32, pb.shape, 0)
    prev = jnp.where(row == 0, carry_ref[...], pltpu.roll(pb, 1, axis=0))
    last = pb[tm - 1:tm, :]
    carry_ref[...] = last
    last_out[0] = last
    pm = pb + (prev - pb) * mu_ref[...]
    r = pm[:, 0:bw]
    k = pm[:, bw:2 * bw]
    v = pm[:, 2 * bw:3 * bw]
    wa = pm[:, 3 * bw:3 * bw + W_LORA + A_LORA]
    gl = pm[:, 3 * bw + W_LORA + A_LORA:]
    ones = ones_ref[...]
    w = -_softplus(-(w0_ref[...] + _dot(jnp.tanh(wa).astype(BF16), wB_ref[...]))) - 0.5
    decay = jnp.exp(-jnp.exp(w))
    a = jax.nn.sigmoid(a0_ref[...] + _dot(wa.astype(BF16), aB_ref[...]))
    g = _dot(jax.nn.sigmoid(gl).astype(BF16), gB_ref[...])
    kk = k * kk_ref[...]
    kk = kk * lax.rsqrt(_segsum(kk * kk, ones) + 1e-12)
    k2 = k * (1.0 + (a - 1.0) * ka_ref[...])
    bonus = _segsum(r * k2 * rk_ref[...], ones)
    r_out[0] = r
    w_out[0] = decay
    k_out[0] = k2
    v_out[0] = v
    kk_out[0] = kk
    b_out[0] = kk * a
    g_out[0] = g
    bv_out[0] = bonus * v


def rwkv_prep(pb3, shift0, mu, w0, wB_pad, a0, aB_pad, gB, k_k, k_a, r_k, ones_bf):
    B, S, PB = pb3.shape
    bw = w0.shape[-1]
    tm = min(256, S)
    row = lambda a: a.reshape(1, -1)
    tok = pl.BlockSpec((1, tm, bw), lambda b, j: (b, j, 0))
    outs = pl.pallas_call(
        _rwkv_prep_body,
        grid=(B, S // tm),
        in_specs=[pl.BlockSpec((1, tm, PB), lambda b, j: (b, j, 0)), pl.BlockSpec((1, 1, PB), lambda b, j: (b, 0, 0)),
                  _full((1, PB)), _full((1, bw)), _full(wB_pad.shape), _full((1, bw)), _full(aB_pad.shape),
                  _full(gB.shape), _full((1, bw)), _full((1, bw)), _full((1, bw)), _full(ones_bf.shape)],
        out_specs=[tok] * 8 + [pl.BlockSpec((1, 1, PB), lambda b, j: (b, 0, 0))],
        out_shape=[jax.ShapeDtypeStruct((B, S, bw), F32)] * 8 + [jax.ShapeDtypeStruct((B, 1, PB), F32)],
        scratch_shapes=[pltpu.VMEM((1, PB), F32)],
        compiler_params=_params("arbitrary", "arbitrary"), name="rwkv_prep")(
            pb3, shift0.reshape(B, 1, PB), row(mu), row(w0), wB_pad, row(a0), aB_pad, gB, row(k_k), row(k_a), row(r_k),
            ones_bf)
    return outs


def _rwkv_scan_body(r_ref, w_ref, k_ref, kk_ref, b_ref, v_ref, s0_ref, y_ref, sl_ref, S_ref, *, tc, nip):
    c = pl.program_id(1)

    @pl.when(c == 0)
    def _():
        S_ref[...] = s0_ref[0]

    def step(t, carry):
        for ip in range(nip):
            Sg = S_ref[ip]
            sa = jnp.sum(Sg * kk_ref[0, t], axis=0, keepdims=True)
            vrow = v_ref[0, t, pl.ds(ip, 1), :]
            Sg = Sg * w_ref[0, t] - sa * b_ref[0, t] + vrow * k_ref[0, t]
            S_ref[ip] = Sg
            y_ref[0, t, pl.ds(ip, 1), :] = jnp.sum(Sg * r_ref[0, t], axis=0, keepdims=True)
        return carry

    lax.fori_loop(0, tc, step, 0)

    @pl.when(c == pl.num_programs(1) - 1)
    def _():
        sl_ref[0] = S_ref[...]


def rwkv_scan(r, w, k, kk, b, v, s0):
    G, S, nb, L = r.shape
    nip = v.shape[2]
    tc = min(64, S)
    vec = pl.BlockSpec((1, tc, nb, L), lambda g, c: (g, c, 0, 0))
    vsp = pl.BlockSpec((1, tc, nip, L), lambda g, c: (g, c, 0, 0))
    ssp = pl.BlockSpec((1, nip, nb, L), lambda g, c: (g, 0, 0, 0))
    return pl.pallas_call(
        functools.partial(_rwkv_scan_body, tc=tc, nip=nip),
        grid=(G, S // tc),
        in_specs=[vec] * 5 + [vsp, ssp],
        out_specs=[vsp, ssp],
        out_shape=[jax.ShapeDtypeStruct((G, S, nip, L), F32), jax.ShapeDtypeStruct((G, nip, nb, L), F32)],
        scratch_shapes=[pltpu.VMEM((nip, nb, L), F32)],
        compiler_params=_params("arbitrary", "arbitrary"), name="rwkv_scan")(r, w, k, kk, b, v, s0)


def _to_scan(x):
    G, Bg, S, _ = x.shape
    t = x.reshape(G, Bg, S, HB, NB).transpose(0, 2, 4, 1, 3).reshape(G, S, NB, Bg * HB)
    return jnp.concatenate([t, t], axis=-1)


def _v_to_scan(x):
    G, Bg, S, _ = x.shape
    return x.reshape(G, Bg, S, HB, 2, NB // 2).transpose(0, 2, 5, 4, 1, 3).reshape(G, S, NB // 2, 2 * Bg * HB)


def _y_from_scan(y, Bg):
    G, S = y.shape[:2]
    return y.reshape(G, S, NB // 2, 2, Bg, HB).transpose(0, 4, 1, 5, 3, 2).reshape(G, Bg, S, HB * NB)


def _state_to_scan(s):
    G, Bg = s.shape[:2]
    return s.reshape(G, Bg, HB, 2, NB // 2, NB).transpose(0, 4, 5, 3, 1, 2).reshape(G, NB // 2, NB, 2 * Bg * HB)


def _state_from_scan(s, Bg):
    G = s.shape[0]
    return s.reshape(G, NB // 2, NB, 2, Bg, HB).transpose(0, 4, 5, 3, 1, 2).reshape(G, Bg, HB, NB, NB)


def _mix0_out_body(ya_ref, y_ref, bv_ref, g_ref, x_ref, gng_ref, gnb_ref, ones_ref, w_ref, o_ref):
    ones = ones_ref[...]
    y = y_ref[...]
    mean = _segsum(y, ones) * (1.0 / NB)
    d = y - mean
    var = _segsum(d * d, ones) * (1.0 / NB)
    yb = (d * lax.rsqrt(var + GN_EPS) * gng_ref[...] + gnb_ref[...] + bv_ref[...]) * g_ref[...]
    aw = ya_ref.shape[1]
    o_ref[...] = x_ref[...] + _dot(ya_ref[...].astype(BF16), w_ref[:aw, :]) + _dot(yb.astype(BF16), w_ref[aw:, :])


def mix0_out(ya, y, bv, g, x, gn_g, gn_b, ones_bf, w_bf):
    T, D = x.shape
    aw = ya.shape[1]
    bw = y.shape[1]
    tm = min(256, T)
    tok = lambda n: pl.BlockSpec((tm, n), lambda i: (i, 0))
    return pl.pallas_call(
        _mix0_out_body,
        grid=(T // tm,),
        in_specs=[tok(aw), tok(bw), tok(bw), tok(bw), tok(D), _full((1, bw)), _full((1, bw)), _full(ones_bf.shape),
                  _full(w_bf.shape)],
        out_specs=tok(D),
        out_shape=jax.ShapeDtypeStruct((T, D), F32),
        compiler_params=_params("arbitrary"), name="mix0_out")(ya, y, bv, g, x, gn_g.reshape(1, bw), gn_b.reshape(1, bw),
                                                              ones_bf, w_bf)


def _mix1_out_body(a1_ref, a2_ref, x_ref, w_ref, o_ref):
    n1 = a1_ref.shape[1]
    o_ref[...] = x_ref[...] + _dot(a1_ref[...].astype(BF16), w_ref[:n1, :]) + _dot(a2_ref[...].astype(BF16), w_ref[n1:, :])


def mix1_out(a1, a2, x, w_bf):
    T, D = x.shape
    tm = min(256, T)
    tok = lambda n: pl.BlockSpec((tm, n), lambda i: (i, 0))
    return pl.pallas_call(
        _mix1_out_body,
        grid=(T // tm,),
        in_specs=[tok(a1.shape[1]), tok(a2.shape[1]), tok(D), _full(w_bf.shape)],
        out_specs=tok(D),
        out_shape=jax.ShapeDtypeStruct((T, D), F32),
        compiler_params=_params("arbitrary"), name="mix1_out")(a1, a2, x, w_bf)


def _xattn_body(x_ref, g_ref, wq_ref, mk_ref, mv_ref, wo_ref, o_ref, *, nh):
    x = x_ref[0]
    h = _rms(x, g_ref[...]).astype(BF16)
    q = _dot(h, wq_ref[...])
    mk = mk_ref[0].astype(BF16)
    mv = mv_ref[0].astype(BF16)
    hd = q.shape[1] // nh
    outs = []
    for hh in range(nh):
        sl = slice(hh * hd, (hh + 1) * hd)
        lg = _dot_nt(q[:, sl].astype(BF16), mk[:, sl]) * (hd ** -0.5)
        e = jnp.exp(lg - jnp.max(lg, -1, keepdims=True))
        p = e / jnp.sum(e, -1, keepdims=True)
        outs.append(_dot(p.astype(BF16), mv[:, sl]))
    o = jnp.concatenate(outs, axis=1).astype(BF16)
    o_ref[0] = x + _dot(o, wo_ref[...])


def cross_attn(x3, g, wq_bf, mk, mv, wo_bf):
    B, S, D = x3.shape
    M = mk.shape[1]
    tm = min(256, S)
    return pl.pallas_call(
        functools.partial(_xattn_body, nh=XH),
        grid=(B, S // tm),
        in_specs=[pl.BlockSpec((1, tm, D), lambda b, j: (b, j, 0)), _full((1, D)), _full(wq_bf.shape),
                  pl.BlockSpec((1, M, D), lambda b, j: (b, 0, 0)), pl.BlockSpec((1, M, D), lambda b, j: (b, 0, 0)),
                  _full(wo_bf.shape)],
        out_specs=pl.BlockSpec((1, tm, D), lambda b, j: (b, j, 0)),
        out_shape=jax.ShapeDtypeStruct((B, S, D), F32),
        compiler_params=_params("arbitrary", "arbitrary"), name="cross_attn")(x3, g.reshape(1, D), wq_bf, mk, mv, wo_bf)


def _ffn_body(x_ref, g_ref, wg_ref, wu_ref, wd_ref, gf_ref, o_ref, *, final_norm):
    x = x_ref[...]
    h = _rms(x, g_ref[...]).astype(BF16)
    gate = _dot(h, wg_ref[...])
    up = _dot(h, wu_ref[...])
    act = (jax.nn.silu(gate) * up).astype(BF16)
    y = x + _dot(act, wd_ref[...])
    if final_norm:
        y = _rms(y, gf_ref[...])
    o_ref[...] = y


def ffn(x, g, wg_bf, wu_bf, wd_bf, g_final, final_norm):
    T, D = x.shape
    tm = min(256, T)
    return pl.pallas_call(
        functools.partial(_ffn_body, final_norm=final_norm),
        grid=(T // tm,),
        in_specs=[pl.BlockSpec((tm, D), lambda i: (i, 0)), _full((1, D)), _full(wg_bf.shape), _full(wu_bf.shape),
                  _full(wd_bf.shape), _full((1, D))],
        out_specs=pl.BlockSpec((tm, D), lambda i: (i, 0)),
        out_shape=jax.ShapeDtypeStruct((T, D), F32),
        compiler_params=_params("arbitrary"), name="ffn")(x, g.reshape(1, D), wg_bf, wu_bf, wd_bf, g_final.reshape(1, D))


def _shift_rows(x, d, fill):
    row = lax.broadcasted_iota(I32, x.shape, 0)
    return jnp.where(row >= d, pltpu.roll(x, d, axis=0), fill)


def _lru_body(pc_ref, h0_ref, cv0_ref, cw_ref, cb_ref, wx_ref, bx_ref, wa_ref, ba_ref, lam_ref,
              y_out, h_out, cv_out, hc_ref, cc_ref, *, reset_first):
    j = pl.program_id(1)
    pc = pc_ref[0]
    tm = pc.shape[0]
    cw = pc.shape[1] // 2
    gate_pre = pc[:, :cw]
    x = pc[:, cw:]

    @pl.when(j == 0)
    def _():
        hc_ref[...] = h0_ref[0]
        cc_ref[...] = cv0_ref[0]

    carry = cc_ref[...]
    row8 = lax.broadcasted_iota(I32, (SUBLANES, cw), 0)
    y = cb_ref[...] + x * cw_ref[CONV_W - 1:CONV_W, :]
    for d in range(1, CONV_W):
        xs = pltpu.roll(x, d, axis=0)
        first = jnp.where(row8 < d, pltpu.roll(carry, d, axis=0), xs[:SUBLANES])
        xd = first if tm == SUBLANES else jnp.concatenate([first, xs[SUBLANES:]], axis=0)
        y = y + xd * cw_ref[CONV_W - 1 - d:CONV_W - d, :]
    new_carry = x[tm - SUBLANES:, :]
    cc_ref[...] = new_carry
    cv_out[0] = new_carry

    yb = y.astype(BF16)
    gx = jax.nn.sigmoid(_dot(yb, wx_ref[...]) + bx_ref[...])
    ga = jax.nn.sigmoid(_dot(yb, wa_ref[...]) + ba_ref[...])
    log_a = LRU_C * ga * (-_softplus(-lam_ref[...]))
    a = jnp.exp(log_a)
    th = jnp.tanh(log_a)
    mult = jnp.sqrt(-2.0 * th / (1.0 - th))
    if reset_first:
        row = lax.broadcasted_iota(I32, (tm, cw), 0)
        mult = jnp.where((row == 0) & (j == 0), 1.0, mult)
    bt = y * gx * mult
    d = 1
    while d < tm:
        bt = a * _shift_rows(bt, d, 0.0) + bt
        a = a * _shift_rows(a, d, 1.0)
        d *= 2
    h = bt + a * hc_ref[...]
    h_last = h[tm - 1:tm, :]
    hc_ref[...] = h_last
    h_out[0] = h_last
    y_out[0] = h * jax.nn.gelu(gate_pre)


def lru_mixer(pc3, h0, conv0, conv_w, conv_b, wx_bd, bx, wa_bd, ba, lam, reset_first):
    B, S, W = pc3.shape
    cw = W // 2
    tm = min(256, S)
    cv0 = jnp.concatenate([jnp.zeros((B, SUBLANES - (CONV_W - 1), cw), F32), conv0], axis=1)
    row = lambda a: a.reshape(1, cw)
    y, h_last, cv = pl.pallas_call(
        functools.partial(_lru_body, reset_first=reset_first),
        grid=(B, S // tm),
        in_specs=[pl.BlockSpec((1, tm, W), lambda b, j: (b, j, 0)), pl.BlockSpec((1, 1, cw), lambda b, j: (b, 0, 0)),
                  pl.BlockSpec((1, SUBLANES, cw), lambda b, j: (b, 0, 0)), _full((CONV_W, cw)), _full((1, cw)),
                  _full(wx_bd.shape), _full((1, cw)), _full(wa_bd.shape), _full((1, cw)), _full((1, cw))],
        out_specs=[pl.BlockSpec((1, tm, cw), lambda b, j: (b, j, 0)), pl.BlockSpec((1, 1, cw), lambda b, j: (b, 0, 0)),
                   pl.BlockSpec((1, SUBLANES, cw), lambda b, j: (b, 0, 0))],
        out_shape=[jax.ShapeDtypeStruct((B, S, cw), F32), jax.ShapeDtypeStruct((B, 1, cw), F32),
                   jax.ShapeDtypeStruct((B, SUBLANES, cw), F32)],
        scratch_shapes=[pltpu.VMEM((1, cw), F32), pltpu.VMEM((SUBLANES, cw), F32)],
        compiler_params=_params("arbitrary", "arbitrary"), name="lru_mixer")(
            pc3, h0.reshape(B, 1, cw), cv0, conv_w, row(conv_b), wx_bd, row(bx), wa_bd, row(ba), row(lam))
    return y, h_last[:, 0], cv[:, SUBLANES - (CONV_W - 1):]


def _rope_tables(pos, head_dim):
    rot = head_dim // 4
    half = rot // 2
    inv = ROPE_THETA ** (-jnp.arange(half, dtype=F32) * 2.0 / rot)
    ang = pos.astype(F32)[:, None] * inv[None, :]
    cos, sin = jnp.cos(ang), jnp.sin(ang)
    S = pos.shape[0]
    rest = head_dim - rot
    c = jnp.concatenate([cos, cos, jnp.ones((S, rest), F32)], -1)
    sa = jnp.concatenate([-sin, jnp.zeros((S, head_dim - half), F32)], -1)
    sb = jnp.concatenate([jnp.zeros((S, half), F32), sin, jnp.zeros((S, rest), F32)], -1)
    rep = LANES // head_dim
    return tuple(jnp.tile(t, (1, rep)) for t in (c, sa, sb))


def _rope(x, c, sa, sb, half):
    n = x.shape[1] // LANES
    tile = (lambda t: t) if n == 1 else (lambda t: jnp.concatenate([t] * n, axis=1))
    w = x.shape[1]
    return x * tile(c) + pltpu.roll(x, w - half, axis=1) * tile(sa) + pltpu.roll(x, half, axis=1) * tile(sb)


def _rope_all(q_ref, k_ref, qi_ref, kiw_ref, cq_ref, saq_ref, sbq_ref, ci_ref, sai_ref, sbi_ref):
    qr = _rope(q_ref[...], cq_ref[...], saq_ref[...], sbq_ref[...], HD // 8)
    kr = _rope(k_ref[...], cq_ref[...], saq_ref[...], sbq_ref[...], HD // 8)
    qir = _rope(qi_ref[...], ci_ref[...], sai_ref[...], sbi_ref[...], D_I // 8)
    kiw = kiw_ref[...]
    lane = lax.broadcasted_iota(I32, kiw.shape, 1)
    iski = lane < D_I
    kiwr = _rope(kiw, jnp.where(iski, ci_ref[...], 1.0), jnp.where(iski, sai_ref[...], 0.0),
                 jnp.where(iski, sbi_ref[...], 0.0), D_I // 8)
    return qr, kr, qir, kiwr


def _dsa_prep_sample_body(q_ref, k_ref, qi_ref, kiw_ref, cq_ref, saq_ref, sbq_ref, ci_ref, sai_ref, sbi_ref,
                          qr_out, kr_out, qir_out, kiwr_out):
    qr, kr, qir, kiwr = _rope_all(q_ref, k_ref, qi_ref, kiw_ref, cq_ref, saq_ref, sbq_ref, ci_ref, sai_ref, sbi_ref)
    qr_out[...] = qr
    kr_out[...] = kr
    qir_out[...] = qir
    kiwr_out[...] = kiwr


def _dsa_prep_prompt_body(q_ref, k_ref, v_ref, qi_ref, kiw_ref, cq_ref, saq_ref, sbq_ref, ci_ref, sai_ref, sbi_ref,
                          kr_out, kiwr_out, wq_out, wi_out, wrow_out, kbf_out, kib_out, vt_out):
    qr, kr, qir, kiwr = _rope_all(q_ref, k_ref, qi_ref, kiw_ref, cq_ref, saq_ref, sbq_ref, ci_ref, sai_ref, sbi_ref)
    tm = qr.shape[0]
    kr_out[...] = kr
    kiwr_out[...] = kiwr
    kbf_out[...] = kr.astype(BF16)
    kib_out[...] = kiwr.astype(BF16)
    vt_out[0] = v_ref[...].T.astype(BF16)
    qt = (qr * (HD ** -0.5)).T
    z = jnp.zeros((HD, tm), F32)
    slabs = []
    for h in range(H_D):
        piece = qt[h * HD:(h + 1) * HD, :]
        slabs.append(jnp.concatenate([piece, z] if h < H_D // N_KV else [z, piece], axis=0))
    wq_out[0] = jnp.concatenate(slabs, axis=1).astype(BF16)
    qit = qir.T
    zi = jnp.zeros((LANES - D_I, tm), F32)
    wi_out[0] = jnp.concatenate([jnp.concatenate([qit[h * D_I:(h + 1) * D_I, :], zi], axis=0) for h in range(H_I)],
                                axis=1).astype(BF16)
    wrow_out[0] = kiwr.T[D_I:D_I + H_I, :] * (H_I ** -0.5 * D_I ** -0.5)


def dsa_prep(q, k, v, qi, kiw, tabs_q, tabs_i, prompt):
    T = q.shape[0]
    tm = Q_BLOCK if prompt else min(256, T)
    tok = lambda n: pl.BlockSpec((tm, n), lambda i: (i, 0))
    nq, nk, ni = q.shape[1], k.shape[1], qi.shape[1]
    tabs = list(tabs_q) + list(tabs_i)
    if not prompt:
        return pl.pallas_call(
            _dsa_prep_sample_body,
            grid=(T // tm,),
            in_specs=[tok(nq), tok(nk), tok(ni), tok(LANES)] + [tok(LANES)] * 6,
            out_specs=[tok(nq), tok(nk), tok(ni), tok(LANES)],
            out_shape=[jax.ShapeDtypeStruct((T, n), F32) for n in (nq, nk, ni, LANES)],
            compiler_params=_params("arbitrary"), name="dsa_prep_sample")(q, k, qi, kiw, *tabs)
    nblk = T // tm
    blk = lambda r, c: pl.BlockSpec((1, r, c), lambda i: (i, 0, 0))
    return pl.pallas_call(
        _dsa_prep_prompt_body,
        grid=(nblk,),
        in_specs=[tok(nq), tok(nk), tok(nk), tok(ni), tok(LANES)] + [tok(LANES)] * 6,
        out_specs=[tok(nk), tok(LANES), blk(nk, H_D * tm), blk(LANES, H_I * tm), blk(H_I, tm), tok(nk), tok(LANES),
                   blk(nk, tm)],
        out_shape=[jax.ShapeDtypeStruct((T, nk), F32), jax.ShapeDtypeStruct((T, LANES), F32),
                   jax.ShapeDtypeStruct((nblk, nk, H_D * tm), BF16), jax.ShapeDtypeStruct((nblk, LANES, H_I * tm), BF16),
                   jax.ShapeDtypeStruct((nblk, H_I, tm), F32), jax.ShapeDtypeStruct((T, nk), BF16),
                   jax.ShapeDtypeStruct((T, LANES), BF16), jax.ShapeDtypeStruct((nblk, nk, tm), BF16)],
        compiler_params=_params("arbitrary"), name="dsa_prep_prompt")(q, k, v, qi, kiw, *tabs)


def _sort_key(x):
    x = jnp.where(x == 0.0, 0.0, x)
    b = pltpu.bitcast(x, I32)
    return b ^ ((b >> 31) & 0x7FFFFFFF)


def _dsa_prompt_body(wq_ref, wi_ref, wrow_ref, kbf_ref, kib_ref, vt_ref, o_ref,
                     keys_ref, acc_ref, m_ref, l_ref, *, topk, kpb):
    i = pl.program_id(1)
    qb = Q_BLOCK
    q0 = i * qb
    nkc = i + 1
    nkb = (q0 + qb + kpb * qb - 1) // (kpb * qb)
    sub = lax.broadcasted_iota(I32, (qb, qb), 0)
    lane = lax.broadcasted_iota(I32, (qb, qb), 1)
    wi = wi_ref[0]
    wrow = wrow_ref[0]

    def score_chunk(c, carry):
        r = _dot(kib_ref[0, pl.ds(c * qb, qb), :], wi)
        acc = wrow[0:1, :] * jnp.maximum(r[:, 0:qb], 0.0)
        for h in range(1, H_I):
            acc = acc + wrow[h:h + 1, :] * jnp.maximum(r[:, h * qb:(h + 1) * qb], 0.0)
        causal = (c * qb + sub) <= (q0 + lane)
        keys_ref[pl.ds(c * qb, qb), :] = jnp.where(causal, _sort_key(acc), INT_MIN)
        return carry

    lax.fori_loop(0, nkb * kpb, score_chunk, 0)

    def count(pred):
        def body(c, cnt):
            m = jnp.where(pred(keys_ref[pl.ds(c * qb, qb), :]), 1, 0)
            return cnt + jnp.sum(m.reshape(qb // SUBLANES, SUBLANES, qb), axis=0)
        cnt = lax.fori_loop(0, nkc, body, jnp.zeros((SUBLANES, qb), I32))
        return jnp.sum(cnt, axis=0, keepdims=True)

    def bisect(it, thr):
        cand = thr + jnp.left_shift(jnp.int32(1), 31 - it)
        return jnp.where(count(lambda kc: kc >= cand) >= topk, cand, thr)

    thr = lax.fori_loop(0, 32, bisect, jnp.full((1, qb), INT_MIN, I32))
    need = (topk - count(lambda kc: kc > thr)).astype(F32)
    excess = (count(lambda kc: kc >= thr) > topk) & (thr > INT_MIN)

    @pl.when(jnp.max(jnp.where(excess, 1, 0)) > 0)
    def _():
        ltri = jnp.where(lane < sub, 1.0, 0.0).astype(BF16)

        def body(c, run):
            kc = keys_ref[pl.ds(c * qb, qb), :]
            eq = kc == thr
            eqf = jnp.where(eq, 1.0, 0.0)
            pre = _dot(ltri, eqf.astype(BF16)) + run
            keys_ref[pl.ds(c * qb, qb), :] = jnp.where(eq & (pre >= need), INT_MIN, kc)
            return run + jnp.sum(eqf, axis=0, keepdims=True)

        lax.fori_loop(0, nkc, body, jnp.zeros((1, qb), F32))

    wq = wq_ref[0]
    nl = H_D * qb
    m_ref[...] = jnp.full((1, nl), -1e30, F32)
    l_ref[...] = jnp.zeros((1, nl), F32)
    acc_ref[...] = jnp.zeros(acc_ref.shape, F32)
    lb = kpb * qb
    hpg = H_D // N_KV

    def attend(kb, carry):
        logits = _dot(kbf_ref[0, pl.ds(kb * lb, lb), :], wq)
        kc = keys_ref[pl.ds(kb * lb, lb), :]
        sel = (kc >= thr) & (kc > INT_MIN)
        ps, alphas = [], []
        for h in range(H_D):
            lg = logits[:, h * qb:(h + 1) * qb]
            m_old = m_ref[:, h * qb:(h + 1) * qb]
            m_new = jnp.maximum(m_old, jnp.max(jnp.where(sel, lg, -1e30), axis=0, keepdims=True))
            p = jnp.where(sel, jnp.exp(lg - m_new), 0.0)
            alpha = jnp.exp(m_old - m_new)
            m_ref[:, h * qb:(h + 1) * qb] = m_new
            l_ref[:, h * qb:(h + 1) * qb] = alpha * l_ref[:, h * qb:(h + 1) * qb] + jnp.sum(p, axis=0, keepdims=True)
            ps.append(p.astype(BF16))
            alphas.append(alpha)
        vt = vt_ref[0, kb]
        for g in range(N_KV):
            pg = jnp.concatenate(ps[g * hpg:(g + 1) * hpg], axis=1)
            ag = jnp.concatenate(alphas[g * hpg:(g + 1) * hpg], axis=1)
            gs = slice(g * hpg * qb, (g + 1) * hpg * qb)
            acc_ref[:, gs] = acc_ref[:, gs] * ag + _dot(vt[g * HD:(g + 1) * HD, :], pg)
        return carry

    lax.fori_loop(0, nkb, attend, 0)
    out_t = acc_ref[...] / l_ref[...]
    o_ref[0] = jnp.concatenate([out_t[:, h * qb:(h + 1) * qb] for h in range(H_D)], axis=0).T


def dsa_prompt(wq, wi, wrow, kbf, kib, vt, B, S):
    nq = S // Q_BLOCK
    topk = min(TOPK_MAX, S // 4)
    kpb = 4 if S % (4 * Q_BLOCK) == 0 else 1
    lb = kpb * Q_BLOCK
    nk = kbf.shape[-1]
    kbf3 = kbf.reshape(B, S, nk)
    kib3 = kib.reshape(B, S, LANES)
    vt4 = vt.reshape(B, S // lb, kpb, nk, Q_BLOCK).transpose(0, 1, 3, 2, 4).reshape(B, S // lb, nk, lb)
    qspec = lambda r, c: pl.BlockSpec((1, r, c), lambda b, i: (b * nq + i, 0, 0))
    return pl.pallas_call(
        functools.partial(_dsa_prompt_body, topk=topk, kpb=kpb),
        grid=(B, nq),
        in_specs=[qspec(nk, H_D * Q_BLOCK), qspec(LANES, H_I * Q_BLOCK), qspec(H_I, Q_BLOCK),
                  pl.BlockSpec((1, S, nk), lambda b, i: (b, 0, 0)), pl.BlockSpec((1, S, LANES), lambda b, i: (b, 0, 0)),
                  pl.BlockSpec((1, S // lb, nk, lb), lambda b, i: (b, 0, 0, 0))],
        out_specs=pl.BlockSpec((1, Q_BLOCK, H_D * HD), lambda b, i: (b, i, 0)),
        out_shape=jax.ShapeDtypeStruct((B, S, H_D * HD), F32),
        scratch_shapes=[pltpu.VMEM((S, Q_BLOCK), I32), pltpu.VMEM((HD, H_D * Q_BLOCK), F32),
                        pltpu.VMEM((1, H_D * Q_BLOCK), F32), pltpu.VMEM((1, H_D * Q_BLOCK), F32)],
        compiler_params=_params("arbitrary", "arbitrary"), name="dsa_prompt")(wq, wi, wrow, kbf3, kib3, vt4)


def _dsa_s_scores_body(pt_ref, qi_ref, wb_ref, pool_ref, new_ref, s_ref, *, npg, t):
    p = pl.program_id(1)
    is_new = p == npg
    page = jnp.where(is_new, new_ref[0], pool_ref[0]).astype(BF16)
    r = _dot_nt(qi_ref[0], page)
    wb = wb_ref[0]
    acc = wb[0:t, :] * jnp.maximum(r[0:t, :], 0.0)
    for h in range(1, H_I):
        acc = acc + wb[h * t:(h + 1) * t, :] * jnp.maximum(r[h * t:(h + 1) * t, :], 0.0)
    sub = lax.broadcasted_iota(I32, acc.shape, 0)
    lane = lax.broadcasted_iota(I32, acc.shape, 1)
    valid = jnp.logical_not(is_new) | (lane <= sub)
    s_ref[0, 0] = jnp.where(valid, _sort_key(acc), INT_MIN)


def _dsa_s_select_body(k_ref, ko_ref, thr_ref, *, topk, nch):
    nb, _, t, ch = k_ref.shape
    rows = nb * t
    chunk = lambda ref, c: ref[:, c].reshape(rows, ch)

    def count(pred):
        def body(c, cnt):
            return cnt + jnp.where(pred(chunk(k_ref, c)), 1, 0)
        cnt = lax.fori_loop(0, nch, body, jnp.zeros((rows, ch), I32))
        return jnp.sum(cnt, axis=1, keepdims=True)

    def bisect(it, thr):
        cand = thr + jnp.left_shift(jnp.int32(1), 31 - it)
        return jnp.where(count(lambda kc: kc >= cand) >= topk, cand, thr)

    thr = lax.fori_loop(0, 32, bisect, jnp.full((rows, 1), INT_MIN, I32))
    need = (topk - count(lambda kc: kc > thr)).astype(F32)
    sub = lax.broadcasted_iota(I32, (ch, ch), 0)
    lane = lax.broadcasted_iota(I32, (ch, ch), 1)
    utri = jnp.where(sub < lane, 1.0, 0.0).astype(BF16)

    def body(c, run):
        kc = chunk(k_ref, c)
        eq = kc == thr
        eqf = jnp.where(eq, 1.0, 0.0)
        pre = _dot(eqf.astype(BF16), utri) + run
        ko_ref[:, c] = jnp.where(eq & (pre >= need), INT_MIN, kc).reshape(nb, t, ch)
        return run + jnp.sum(eqf, axis=1, keepdims=True)

    lax.fori_loop(0, nch, body, jnp.zeros((rows, 1), F32))
    thr_ref[...] = jnp.broadcast_to(thr, (rows, ch)).reshape(nb, t, ch)


def _dsa_s_attend_body(pt_ref, qm_ref, kpool_ref, vpool_ref, knew_ref, vnew_ref, keys_ref, thr_ref, o_ref,
                       m_ref, l_ref, acc_ref, *, npg, t):
    p = pl.program_id(1)

    @pl.when(p == 0)
    def _():
        m_ref[...] = jnp.full(m_ref.shape, -1e30, F32)
        l_ref[...] = jnp.zeros(l_ref.shape, F32)
        acc_ref[...] = jnp.zeros(acc_ref.shape, F32)

    is_new = p == npg
    kp = jnp.where(is_new, knew_ref[0], kpool_ref[0]).astype(BF16)
    vp = jnp.where(is_new, vnew_ref[0], vpool_ref[0]).astype(BF16)
    lg = _dot_nt(qm_ref[0], kp)
    kc = jnp.concatenate([keys_ref[0, 0]] * H_D, axis=0)
    sel = (kc >= jnp.concatenate([thr_ref[0]] * H_D, axis=0)) & (kc > INT_MIN)
    m_old = m_ref[...]
    m_new = jnp.maximum(m_old, jnp.max(jnp.where(sel, lg, -1e30), axis=1, keepdims=True))
    pr = jnp.where(sel, jnp.exp(lg - m_new), 0.0)
    alpha = jnp.exp(m_old - m_new)
    m_ref[...] = m_new
    l_ref[...] = alpha * l_ref[...] + jnp.sum(pr, axis=1, keepdims=True)
    acc_ref[...] = alpha * acc_ref[...] + _dot(pr.astype(BF16), vp)

    @pl.when(is_new)
    def _():
        o_ref[0] = acc_ref[...] / l_ref[...]


def dsa_sample(qr, kr, v, qir, kiwr, pool_k, pool_v, pool_ki, page_table, B, T):
    npg = page_table.shape[1]
    npool = pool_k.shape[0]
    nkv = N_KV * HD
    topk = min(TOPK_MAX, (npg * PAGE_SIZE + T) // 4)
    pad_new = lambda a: jnp.pad(a.reshape(B, T, -1), ((0, 0), (0, PAGE_SIZE - T), (0, 0)))
    ki_new = pad_new(kiwr[:, :D_I])
    k_new = pad_new(kr)
    v_new = pad_new(v)
    qi_m = qir.reshape(B, T, H_I, D_I).transpose(0, 2, 1, 3).reshape(B, H_I * T, D_I).astype(BF16)
    wi = kiwr[:, D_I:D_I + H_I].reshape(B, T, H_I) * (H_I ** -0.5 * D_I ** -0.5)
    wb = jnp.broadcast_to(wi.transpose(0, 2, 1).reshape(B, H_I * T, 1), (B, H_I * T, PAGE_SIZE))
    pool_page = lambda w: pl.BlockSpec((1, PAGE_SIZE, w), lambda b, p, pt: (pt[b, jnp.minimum(p, npg - 1)], 0, 0))
    per_b = lambda r, c: pl.BlockSpec((1, r, c), lambda b, p, pt: (b, 0, 0))
    per_bp = lambda r: pl.BlockSpec((1, 1, r, PAGE_SIZE), lambda b, p, pt: (b, p, 0, 0))
    keys = pl.pallas_call(
        functools.partial(_dsa_s_scores_body, npg=npg, t=T),
        grid_spec=pltpu.PrefetchScalarGridSpec(
            num_scalar_prefetch=1, grid=(B, npg + 1),
            in_specs=[per_b(H_I * T, D_I), per_b(H_I * T, PAGE_SIZE), pool_page(D_I), per_b(PAGE_SIZE, D_I)],
            out_specs=per_bp(T)),
        out_shape=jax.ShapeDtypeStruct((B, npg + 1, T, PAGE_SIZE), I32),
        compiler_params=_params("arbitrary", "arbitrary"), name="dsa_sample_scores")(
            page_table, qi_m, wb, pool_ki, ki_new)
    nb = 32 // T if B % (32 // T) == 0 else B
    kblk = pl.BlockSpec((nb, npg + 1, T, PAGE_SIZE), lambda i: (i, 0, 0, 0))
    tblk = pl.BlockSpec((nb, T, PAGE_SIZE), lambda i: (i, 0, 0))
    keys2, thr = pl.pallas_call(
        functools.partial(_dsa_s_select_body, topk=topk, nch=npg + 1),
        grid=(B // nb,),
        in_specs=[kblk],
        out_specs=[kblk, tblk],
        out_shape=[jax.ShapeDtypeStruct(keys.shape, I32), jax.ShapeDtypeStruct((B, T, PAGE_SIZE), I32)],
        compiler_params=_params("arbitrary"), name="dsa_sample_select")(keys)
    q4 = (qr * (HD ** -0.5)).reshape(B, T, N_KV, H_D // N_KV, HD).transpose(0, 2, 3, 1, 4)
    eye = jnp.eye(N_KV, dtype=F32)
    qm = (q4[:, :, :, :, None, :] * eye[None, :, None, None, :, None]).reshape(B, H_D * T, nkv).astype(BF16)
    out = pl.pallas_call(
        functools.partial(_dsa_s_attend_body, npg=npg, t=T),
        grid_spec=pltpu.PrefetchScalarGridSpec(
            num_scalar_prefetch=1, grid=(B, npg + 1),
            in_specs=[per_b(H_D * T, nkv), pool_page(nkv), pool_page(nkv), per_b(PAGE_SIZE, nkv), per_b(PAGE_SIZE, nkv),
                      per_bp(T), per_b(T, PAGE_SIZE)],
            out_specs=per_b(H_D * T, nkv),
            scratch_shapes=[pltpu.VMEM((H_D * T, 1), F32), pltpu.VMEM((H_D * T, 1), F32),
                            pltpu.VMEM((H_D * T, nkv), F32)]),
        out_shape=jax.ShapeDtypeStruct((B, H_D * T, nkv), F32),
        compiler_params=_params("arbitrary", "arbitrary"), name="dsa_sample_attend")(
            page_table, qm, pool_k.reshape(npool, PAGE_SIZE, nkv), pool_v.reshape(npool, PAGE_SIZE, nkv), k_new, v_new,
            keys2, thr)
    o5 = out.reshape(B, N_KV, H_D // N_KV, T, N_KV, HD)
    og = jnp.stack([o5[:, g, :, :, g, :] for g in range(N_KV)], axis=1)
    return og.transpose(0, 3, 1, 2, 4).reshape(B * T, H_D * HD)


def _block_diag(w):
    n, a, b = w.shape
    eye = jnp.eye(n, dtype=w.dtype)
    return (w[:, :, None, :] * eye[:, None, :, None]).reshape(n * a, n * b)


def _prep_weights(P):
    bf = lambda a: a.astype(BF16)
    bw = HB * NB
    pb_w = 3 * bw + W_LORA + A_LORA + G_LORA
    perm = np.concatenate([np.arange(0, bw), np.arange(bw + W_LORA, 3 * bw + W_LORA), np.arange(bw, bw + W_LORA),
                           np.arange(3 * bw + W_LORA, pb_w)])
    W = {"perm": perm, "inv_perm": np.argsort(perm)}
    aw2 = P["e_w_in"].shape[-1] - pb_w
    W["e_w_in"] = [bf(jnp.concatenate([w[:, :aw2], w[:, aw2:][:, perm]], axis=1)) for w in P["e_w_in"]]
    W["b_mu"] = [m[perm] for m in P["b_mu"]]
    W["wB_pad"] = [bf(jnp.concatenate([w, jnp.zeros((A_LORA, bw), F32)], 0)) for w in P["b_wB"]]
    W["aB_pad"] = [bf(jnp.concatenate([jnp.zeros((W_LORA, bw), F32), w], 0)) for w in P["b_aB"]]
    W["gB"] = [bf(w) for w in P["b_gB"]]
    W["ones_bd"] = bf(_block_diag(jnp.ones((HB, NB, NB), F32)))
    o_in = []
    for w in P["o_w_in"]:
        n_main = w.shape[1] - (D_I + H_I)
        o_in.append(bf(jnp.concatenate([w, jnp.zeros((w.shape[0], LANES - (D_I + H_I)), F32)], axis=1)))
        assert n_main % LANES == 0
    W["o_w_in"] = o_in
    W["wx_bd"] = [bf(_block_diag(w)) for w in P["c_wx"]]
    W["wa_bd"] = [bf(_block_diag(w)) for w in P["c_wa"]]
    for name in ("e_w_out", "o_w_out", "w_xq", "w_xo", "w_ff_gate", "w_ff_up", "w_ff_down"):
        W[name] = [bf(w) for w in P[name]]
    W["w_xkv"] = [bf(jnp.concatenate([k, v], axis=1)) for k, v in zip(P["w_xk"], P["w_xv"])]
    return W


def _memory_kv(mem, norm_mem, W):
    B, M, D = mem.shape
    ks, vs = [], []
    for layer in range(len(W["w_xkv"])):
        k, v = rms_matmul(mem.reshape(B * M, D), norm_mem[layer], W["w_xkv"][layer], (D, D))
        ks.append(k.reshape(B, M, XH, D // XH))
        vs.append(v.reshape(B, M, XH, D // XH))
    return jnp.stack(ks, 0), jnp.stack(vs, 0)


def _trunk(x3, pos0, rwkv_s0, rwkv_shift0, lru_h0, lru_conv0, mem_k, mem_v, P, W, sample_ctx):
    B, S, D = x3.shape
    T = B * S
    prompt = sample_ctx is None
    depth = P["norm_mix"].shape[0]
    x = x3.reshape(T, D)
    pos = pos0 + jnp.arange(S, dtype=I32)
    out = {k: [] for k in ("chunk_v", "rwkv_s", "rwkv_sh", "lru_h", "lru_cv", "att_k", "att_v", "att_ki")}
    bw = HB * NB
    for layer in range(depth):
        j = layer // 2
        if layer % 2 == 0:
            aw2 = W["e_w_in"][j].shape[1] - (3 * bw + W_LORA + A_LORA + G_LORA)
            pa, pb = rms_matmul(x, P["norm_mix"][layer], W["e_w_in"][j], (aw2, W["e_w_in"][j].shape[1] - aw2))
            tril = jnp.tril(P["a_ws"][j])
            if S % CHUNK == 0:
                mix, bias = tril, P["a_bs"][j]
            else:
                assert S <= CHUNK
                eye = jnp.eye(B, dtype=F32)
                mix = (tril[:, None, :S, None, :S] * eye[None, :, None, :, None]).reshape(A_G, T, T)
                bias = jnp.tile(P["a_bs"][j][:, :S], (1, B))
            ac = aw2 // 2 // A_G
            bias_full = jnp.broadcast_to(bias[:, :, None], bias.shape + (ac,))
            ya, v_rows = chunk_mixer(pa, P["a_ln_g"][j], P["a_ln_b"][j], mix.astype(BF16), bias_full)
            pbw = pb.shape[1]
            sh0 = rwkv_shift0[j][:, W["perm"]]
            r, dec, k2, v, kk, bb, g, bv, last = rwkv_prep(
                pb.reshape(B, S, pbw), sh0, W["b_mu"][j], P["b_w0"][j], W["wB_pad"][j], P["b_a0"][j], W["aB_pad"][j],
                W["gB"][j], P["b_kk"][j], P["b_ka"][j], P["b_rk"][j].reshape(-1), W["ones_bd"])
            Bg = LANES // (2 * HB)
            G = B // Bg
            grp = lambda a: a.reshape(G, Bg, S, bw)
            y_s, s_last = rwkv_scan(_to_scan(grp(r)), _to_scan(grp(dec)), _to_scan(grp(k2)), _to_scan(grp(kk)),
                                    _to_scan(grp(bb)), _v_to_scan(grp(v)),
                                    _state_to_scan(rwkv_s0[j].astype(F32).reshape(G, Bg, HB, NB, NB)))
            y = _y_from_scan(y_s, Bg).reshape(T, bw)
            x = mix0_out(ya, y, bv.reshape(T, bw), g.reshape(T, bw), x, P["b_gn_g"][j], P["b_gn_b"][j], W["ones_bd"],
                         W["e_w_out"][j])
            out["chunk_v"].append(v_rows.reshape(B, S, -1))
            out["rwkv_s"].append(_state_from_scan(s_last, Bg).reshape(B, HB, NB, NB))
            out["rwkv_sh"].append(last[:, 0][:, W["inv_perm"]])
        else:
            cw2 = 2 * P["c_conv_b"].shape[-1]
            nq, nk, ni = H_D * HD, N_KV * HD, H_I * D_I
            pc, q, k, v, qi, kiw = rms_matmul(x, P["norm_mix"][layer], W["o_w_in"][j], (cw2, nq, nk, nk, ni, LANES))
            yc, h_last, conv_new = lru_mixer(pc.reshape(B, S, cw2), lru_h0[j], lru_conv0[j], P["c_conv_w"][j],
                                             P["c_conv_b"][j], W["wx_bd"][j], P["c_bx"][j], W["wa_bd"][j], P["c_ba"][j],
                                             P["c_lambda"][j], reset_first=prompt)
            tabs_q = tuple(jnp.tile(t, (B, 1)) for t in _rope_tables(pos, HD))
            tabs_i = tuple(jnp.tile(t, (B, 1)) for t in _rope_tables(pos, D_I))
            if prompt:
                kr, kiwr, wq, wi, wrow, kbf, kib, vt = dsa_prep(q, k, v, qi, kiw, tabs_q, tabs_i, True)
                yd = dsa_prompt(wq, wi, wrow, kbf, kib, vt, B, S).reshape(T, nq)
            else:
                qr, kr, qir, kiwr = dsa_prep(q, k, v, qi, kiw, tabs_q, tabs_i, False)
                pool_k, pool_v, pool_ki, page_table = sample_ctx
                yd = dsa_sample(qr, kr, v, qir, kiwr, pool_k[j], pool_v[j], pool_ki[j], page_table, B, S)
            x = mix1_out(yc.reshape(T, -1), yd, x, W["o_w_out"][j])
            out["lru_h"].append(h_last)
            out["lru_cv"].append(conv_new)
            out["att_k"].append(kr.reshape(B, S, N_KV, HD))
            out["att_v"].append(v.reshape(B, S, N_KV, HD))
            out["att_ki"].append(kiwr[:, :D_I].reshape(B, S, D_I))
        M = mem_k.shape[2]
        x = cross_attn(x.reshape(B, S, D), P["norm_x"][layer], W["w_xq"][layer], mem_k[layer].reshape(B, M, D),
                       mem_v[layer].reshape(B, M, D), W["w_xo"][layer]).reshape(T, D)
        x = ffn(x, P["norm_ffn"][layer], W["w_ff_gate"][layer], W["w_ff_up"][layer], W["w_ff_down"][layer],
                P["norm_final"], final_norm=(layer == depth - 1))
    st = lambda l: jnp.stack(l, 0)
    return (x.reshape(B, S, D),) + tuple(st(out[k]) for k in
                                         ("chunk_v", "rwkv_s", "rwkv_sh", "lru_h", "lru_cv", "att_k", "att_v", "att_ki"))


def kernel(x_prompt, x_sample, mem_prompt, state_rwkv, state_rwkv_shift, state_lru_h, state_lru_conv, cache_attn_k, cache_attn_v, cache_attn_kidx, cache_mem_k, cache_mem_v, page_table, norm_mix, norm_x, norm_mem, norm_ffn, norm_final, w_xq, w_xk, w_xv, w_xo, w_ff_gate, w_ff_up, w_ff_down, e_w_in, e_w_out, a_ln_g, a_ln_b, a_ws, a_bs, b_mu, b_w0, b_wB, b_a0, b_aB, b_gB, b_kk, b_ka, b_rk, b_gn_g, b_gn_b, o_w_in, o_w_out, c_conv_w, c_conv_b, c_wx, c_bx, c_wa, c_ba, c_lambda):
    P = dict(norm_mix=norm_mix, norm_x=norm_x, norm_ffn=norm_ffn, norm_final=norm_final,
             w_xq=w_xq, w_xk=w_xk, w_xv=w_xv, w_xo=w_xo, w_ff_gate=w_ff_gate, w_ff_up=w_ff_up, w_ff_down=w_ff_down,
             e_w_in=e_w_in, e_w_out=e_w_out, a_ln_g=a_ln_g, a_ln_b=a_ln_b, a_ws=a_ws, a_bs=a_bs,
             b_mu=b_mu, b_w0=b_w0, b_wB=b_wB, b_a0=b_a0, b_aB=b_aB, b_gB=b_gB, b_kk=b_kk, b_ka=b_ka,
             b_rk=b_rk, b_gn_g=b_gn_g, b_gn_b=b_gn_b,
             o_w_in=o_w_in, o_w_out=o_w_out, c_conv_w=c_conv_w, c_conv_b=c_conv_b,
             c_wx=c_wx, c_bx=c_bx, c_wa=c_wa, c_ba=c_ba, c_lambda=c_lambda)
    W = _prep_weights(P)
    dt = x_prompt.dtype
    bp = x_prompt.shape[0]
    n_even, n_odd = state_rwkv.shape[0], state_lru_h.shape[0]
    pb_w = state_rwkv_shift.shape[-1]
    cw = state_lru_h.shape[-1]

    p_mem_k, p_mem_v = _memory_kv(mem_prompt, norm_mem, W)
    (y_prompt, _, p_rwkv_state, p_rwkv_shift, p_lru_h, p_lru_conv, p_attn_k, p_attn_v, p_attn_kidx) = _trunk(
        x_prompt, 0,
        jnp.zeros((n_even, bp, HB, NB, NB), dt), jnp.zeros((n_even, bp, pb_w), dt),
        jnp.zeros((n_odd, bp, cw), dt), jnp.zeros((n_odd, bp, CONV_W - 1, cw), dt),
        p_mem_k, p_mem_v, P, W, None)

    past_len = page_table.shape[1] * PAGE_SIZE
    (y_sample, s_chunk_v, s_rwkv_state, s_rwkv_shift, s_lru_h, s_lru_conv, s_attn_k, s_attn_v, s_attn_kidx) = _trunk(
        x_sample, past_len, state_rwkv, state_rwkv_shift, state_lru_h, state_lru_conv,
        cache_mem_k, cache_mem_v, P, W, (cache_attn_k, cache_attn_v, cache_attn_kidx, page_table))

    return (y_prompt, y_sample,
            p_rwkv_state, p_rwkv_shift, p_lru_h, p_lru_conv, p_attn_k, p_attn_v, p_attn_kidx, p_mem_k, p_mem_v,
            s_chunk_v, s_rwkv_state, s_rwkv_shift, s_lru_h, s_lru_conv, s_attn_k, s_attn_v, s_attn_kidx)
```

```python
import functools

import numpy as np
import jax
import jax.numpy as jnp
from jax import lax
from jax.experimental import pallas as pl
from jax.experimental.pallas import tpu as pltpu

F32 = jnp.float32
BF16 = jnp.bfloat16
I32 = jnp.int32
INT_MIN = -2 ** 31

EPS = 1e-6
GN_EPS = 64e-5
CHUNK = 128
A_G = 4
HB = 8
NB = 64
W_LORA = 64
A_LORA = 64
G_LORA = 128
NBLK = 8
CONV_W = 4
LRU_C = 8.0
H_D = 8
HD = 64
N_KV = 2
H_I = 8
D_I = 32
TOPK_MAX = 256
Q_BLOCK = 128
PAGE_SIZE = 128
ROPE_THETA = 500000.0
XH = 4
LANES = 128
SUBLANES = 8
VMEM_LIMIT = 56 * 1024 * 1024
NEG = -1e30
LOG2E = 1.4426950408889634
KEY_NEG_INF = -2139095041


def _params(*sem):
    return pltpu.CompilerParams(dimension_semantics=sem, vmem_limit_bytes=VMEM_LIMIT)


def _dot(a, b):
    return jnp.dot(a, b, preferred_element_type=F32)


def _dot_nt(a, b):
    return lax.dot_general(a, b, (((1,), (1,)), ((), ())), preferred_element_type=F32)


def _rms(x, g):
    return x * lax.rsqrt(jnp.mean(x * x, axis=-1, keepdims=True) + EPS) * g


def _segsum(x, ones_bf):
    hi = x.astype(BF16)
    lo = (x - hi.astype(F32)).astype(BF16)
    return _dot(hi, ones_bf) + _dot(lo, ones_bf)


def _softplus(x):
    return jnp.maximum(x, 0.0) + jnp.log1p(jnp.exp(-jnp.abs(x)))


def _full(shape):
    n = len(shape)
    return pl.BlockSpec(shape, lambda *_: (0,) * n)


def _rms_matmul_body(x_ref, g_ref, w_ref, *out_refs, splits):
    h = _rms(x_ref[...], g_ref[...]).astype(BF16)
    off = 0
    for o_ref, n in zip(out_refs, splits):
        o_ref[...] = _dot(h, w_ref[:, off:off + n])
        off += n


def rms_matmul(x, g, w_bf, splits):
    T, D = x.shape
    N = w_bf.shape[1]
    assert sum(splits) == N
    tm = min(256, T)
    return pl.pallas_call(
        functools.partial(_rms_matmul_body, splits=tuple(splits)),
        grid=(T // tm,),
        in_specs=[pl.BlockSpec((tm, D), lambda i: (i, 0)), _full((1, D)), _full((D, N))],
        out_specs=[pl.BlockSpec((tm, n), lambda i: (i, 0)) for n in splits],
        out_shape=[jax.ShapeDtypeStruct((T, n), F32) for n in splits],
        compiler_params=_params("arbitrary"), name="rms_matmul")(x, g.reshape(1, D), w_bf)


def _chunk_mixer_body(pa_ref, lng_ref, lnb_ref, m_ref, bias_ref, ya_ref, v_ref, *, ng):
    z = jax.nn.gelu(pa_ref[...])
    aw = z.shape[1] // 2
    u = z[:, :aw]
    v = z[:, aw:]
    mu = jnp.mean(v, -1, keepdims=True)
    var = jnp.mean(jnp.square(v - mu), -1, keepdims=True)
    v = (v - mu) * lax.rsqrt(var + EPS) * lng_ref[...] + lnb_ref[...]
    v_ref[...] = v
    vb = v.astype(BF16)
    ac = aw // ng
    for g in range(ng):
        s = _dot(m_ref[g], vb[:, g * ac:(g + 1) * ac]) + bias_ref[g]
        ya_ref[:, g * ac:(g + 1) * ac] = u[:, g * ac:(g + 1) * ac] * s


def chunk_mixer(pa, ln_g, ln_b, mix_bf, bias):
    T, W = pa.shape
    aw = W // 2
    ng, tm, _ = mix_bf.shape
    return pl.pallas_call(
        functools.partial(_chunk_mixer_body, ng=ng),
        grid=(T // tm,),
        in_specs=[pl.BlockSpec((tm, W), lambda i: (i, 0)), _full((1, aw)), _full((1, aw)),
                  _full(mix_bf.shape), _full(bias.shape)],
        out_specs=[pl.BlockSpec((tm, aw), lambda i: (i, 0))] * 2,
        out_shape=[jax.ShapeDtypeStruct((T, aw), F32)] * 2,
        compiler_params=_params("arbitrary"), name="chunk_mixer")(pa, ln_g.reshape(1, aw), ln_b.reshape(1, aw), mix_bf, bias)


def _rwkv_prep_body(pb_ref, sh0_ref, mu_ref, w0_ref, wB_ref, a0_ref, aB_ref, gB_ref, kk_ref, ka_ref, rk_ref, ones_ref,
                    r_out, w_out, k_out, v_out, kk_out, b_out, g_out, bv_out, last_out, carry_ref):
    j = pl.program_id(1)
    pb = pb_ref[0]
    tm = pb.shape[0]
    bw = r_out.shape[2]

    @pl.when(j == 0)
    def _():
        carry_ref[...] = sh0_ref[0]

    row = lax.broadcasted_iota(I32, pb.shape, 0)
    prev = jnp.where(row == 0, carry_ref[...], pltpu.roll(pb, 1, axis=0))
    last = pb[tm - 1:tm, :]
    carry_ref[...] = last
    last_out[0] = last
    pm = pb + (prev - pb) * mu_ref[...]
    r = pm[:, 0:bw]
    k = pm[:, bw:2 * bw]
    v = pm[:, 2 * bw:3 * bw]
    wa = pm[:, 3 * bw:3 * bw + W_LORA + A_LORA]
    gl = pm[:, 3 * bw + W_LORA + A_LORA:]
    ones = ones_ref[...]
    w = -_softplus(-(w0_ref[...] + _dot(jnp.tanh(wa).astype(BF16), wB_ref[...]))) - 0.5
    decay = jnp.exp(-jnp.exp(w))
    a = jax.nn.sigmoid(a0_ref[...] + _dot(wa.astype(BF16), aB_ref[...]))
    g = _dot(jax.nn.sigmoid(gl).astype(BF16), gB_ref[...])
    kk = k * kk_ref[...]
    kk = kk * lax.rsqrt(_segsum(kk * kk, ones) + 1e-12)
    k2 = k * (1.0 + (a - 1.0) * ka_ref[...])
    bonus = _segsum(r * k2 * rk_ref[...], ones)
    r_out[0] = r
    w_out[0] = decay
    k_out[0] = k2
    v_out[0] = v
    kk_out[0] = kk
    b_out[0] = kk * a
    g_out[0] = g
    bv_out[0] = bonus * v


def rwkv_prep(pb3, shift0, mu, w0, wB_pad, a0, aB_pad, gB, k_k, k_a, r_k, ones_bf):
    B, S, PB = pb3.shape
    bw = w0.shape[-1]
    tm = min(256, S)
    row = lambda a: a.reshape(1, -1)
    tok = pl.BlockSpec((1, tm, bw), lambda b, j: (b, j, 0))
    outs = pl.pallas_call(
        _rwkv_prep_body,
        grid=(B, S // tm),
        in_specs=[pl.BlockSpec((1, tm, PB), lambda b, j: (b, j, 0)), pl.BlockSpec((1, 1, PB), lambda b, j: (b, 0, 0)),
                  _full((1, PB)), _full((1, bw)), _full(wB_pad.shape), _full((1, bw)), _full(aB_pad.shape),
                  _full(gB.shape), _full((1, bw)), _full((1, bw)), _full((1, bw)), _full(ones_bf.shape)],
        out_specs=[tok] * 8 + [pl.BlockSpec((1, 1, PB), lambda b, j: (b, 0, 0))],
        out_shape=[jax.ShapeDtypeStruct((B, S, bw), F32)] * 8 + [jax.ShapeDtypeStruct((B, 1, PB), F32)],
        scratch_shapes=[pltpu.VMEM((1, PB), F32)],
        compiler_params=_params("arbitrary", "arbitrary"), name="rwkv_prep")(
            pb3, shift0.reshape(B, 1, PB), row(mu), row(w0), wB_pad, row(a0), aB_pad, gB, row(k_k), row(k_a), row(r_k),
            ones_bf)
    return outs


def _rwkv_scan_body(r_ref, w_ref, k_ref, kk_ref, b_ref, v_ref, s0_ref, y_ref, sl_ref, S_ref, *, tc, nip):
    c = pl.program_id(1)

    @pl.when(c == 0)
    def _():
        S_ref[...] = s0_ref[0]

    def step(t, carry):
        for ip in range(nip):
            Sg = S_ref[ip]
            sa = jnp.sum(Sg * kk_ref[0, t], axis=0, keepdims=True)
            vrow = v_ref[0, t, pl.ds(ip, 1), :]
            Sg = Sg * w_ref[0, t] - sa * b_ref[0, t] + vrow * k_ref[0, t]
            S_ref[ip] = Sg
            y_ref[0, t, pl.ds(ip, 1), :] = jnp.sum(Sg * r_ref[0, t], axis=0, keepdims=True)
        return carry

    lax.fori_loop(0, tc, step, 0)

    @pl.when(c == pl.num_programs(1) - 1)
    def _():
        sl_ref[0] = S_ref[...]


def rwkv_scan(r, w, k, kk, b, v, s0):
    G, S, nb, L = r.shape
    nip = v.shape[2]
    tc = min(64, S)
    vec = pl.BlockSpec((1, tc, nb, L), lambda g, c: (g, c, 0, 0))
    vsp = pl.BlockSpec((1, tc, nip, L), lambda g, c: (g, c, 0, 0))
    ssp = pl.BlockSpec((1, nip, nb, L), lambda g, c: (g, 0, 0, 0))
    return pl.pallas_call(
        functools.partial(_rwkv_scan_body, tc=tc, nip=nip),
        grid=(G, S // tc),
        in_specs=[vec] * 5 + [vsp, ssp],
        out_specs=[vsp, ssp],
        out_shape=[jax.ShapeDtypeStruct((G, S, nip, L), F32), jax.ShapeDtypeStruct((G, nip, nb, L), F32)],
        scratch_shapes=[pltpu.VMEM((nip, nb, L), F32)],
        compiler_params=_params("arbitrary", "arbitrary"), name="rwkv_scan")(r, w, k, kk, b, v, s0)


def _to_scan(x):
    G, Bg, S, _ = x.shape
    t = x.reshape(G, Bg, S, HB, NB).transpose(0, 2, 4, 1, 3).reshape(G, S, NB, Bg * HB)
    return jnp.concatenate([t, t], axis=-1)


def _v_to_scan(x):
    G, Bg, S, _ = x.shape
    return x.reshape(G, Bg, S, HB, 2, NB // 2).transpose(0, 2, 5, 4, 1, 3).reshape(G, S, NB // 2, 2 * Bg * HB)


def _y_from_scan(y, Bg):
    G, S = y.shape[:2]
    return y.reshape(G, S, NB // 2, 2, Bg, HB).transpose(0, 4, 1, 5, 3, 2).reshape(G, Bg, S, HB * NB)


def _state_to_scan(s):
    G, Bg = s.shape[:2]
    return s.reshape(G, Bg, HB, 2, NB // 2, NB).transpose(0, 4, 5, 3, 1, 2).reshape(G, NB // 2, NB, 2 * Bg * HB)


def _state_from_scan(s, Bg):
    G = s.shape[0]
    return s.reshape(G, NB // 2, NB, 2, Bg, HB).transpose(0, 4, 5, 3, 1, 2).reshape(G, Bg, HB, NB, NB)


def _mix0_out_body(ya_ref, y_ref, bv_ref, g_ref, x_ref, gng_ref, gnb_ref, ones_ref, w_ref, o_ref):
    ones = ones_ref[...]
    y = y_ref[...]
    mean = _segsum(y, ones) * (1.0 / NB)
    d = y - mean
    var = _segsum(d * d, ones) * (1.0 / NB)
    yb = (d * lax.rsqrt(var + GN_EPS) * gng_ref[...] + gnb_ref[...] + bv_ref[...]) * g_ref[...]
    aw = ya_ref.shape[1]
    o_ref[...] = x_ref[...] + _dot(ya_ref[...].astype(BF16), w_ref[:aw, :]) + _dot(yb.astype(BF16), w_ref[aw:, :])


def mix0_out(ya, y, bv, g, x, gn_g, gn_b, ones_bf, w_bf):
    T, D = x.shape
    aw = ya.shape[1]
    bw = y.shape[1]
    tm = min(256, T)
    tok = lambda n: pl.BlockSpec((tm, n), lambda i: (i, 0))
    return pl.pallas_call(
        _mix0_out_body,
        grid=(T // tm,),
        in_specs=[tok(aw), tok(bw), tok(bw), tok(bw), tok(D), _full((1, bw)), _full((1, bw)), _full(ones_bf.shape),
                  _full(w_bf.shape)],
        out_specs=tok(D),
        out_shape=jax.ShapeDtypeStruct((T, D), F32),
        compiler_params=_params("arbitrary"), name="mix0_out")(ya, y, bv, g, x, gn_g.reshape(1, bw), gn_b.reshape(1, bw),
                                                              ones_bf, w_bf)


def _mix1_out_body(a1_ref, a2_ref, x_ref, w_ref, o_ref):
    n1 = a1_ref.shape[1]
    o_ref[...] = x_ref[...] + _dot(a1_ref[...].astype(BF16), w_ref[:n1, :]) + _dot(a2_ref[...].astype(BF16), w_ref[n1:, :])


def mix1_out(a1, a2, x, w_bf):
    T, D = x.shape
    tm = min(256, T)
    tok = lambda n: pl.BlockSpec((tm, n), lambda i: (i, 0))
    return pl.pallas_call(
        _mix1_out_body,
        grid=(T // tm,),
        in_specs=[tok(a1.shape[1]), tok(a2.shape[1]), tok(D), _full(w_bf.shape)],
        out_specs=tok(D),
        out_shape=jax.ShapeDtypeStruct((T, D), F32),
        compiler_params=_params("arbitrary"), name="mix1_out")(a1, a2, x, w_bf)


def _xattn_body(x_ref, g_ref, wq_ref, mk_ref, mv_ref, wo_ref, o_ref, *, nh):
    x = x_ref[0]
    h = _rms(x, g_ref[...]).astype(BF16)
    q = _dot(h, wq_ref[...])
    mk = mk_ref[0].astype(BF16)
    mv = mv_ref[0].astype(BF16)
    hd = q.shape[1] // nh
    outs = []
    for hh in range(nh):
        sl = slice(hh * hd, (hh + 1) * hd)
        lg = _dot_nt(q[:, sl].astype(BF16), mk[:, sl]) * (hd ** -0.5)
        e = jnp.exp(lg - jnp.max(lg, -1, keepdims=True))
        p = e / jnp.sum(e, -1, keepdims=True)
        outs.append(_dot(p.astype(BF16), mv[:, sl]))
    o = jnp.concatenate(outs, axis=1).astype(BF16)
    o_ref[0] = x + _dot(o, wo_ref[...])


def cross_attn(x3, g, wq_bf, mk, mv, wo_bf):
    B, S, D = x3.shape
    M = mk.shape[1]
    tm = min(256, S)
    return pl.pallas_call(
        functools.partial(_xattn_body, nh=XH),
        grid=(B, S // tm),
        in_specs=[pl.BlockSpec((1, tm, D), lambda b, j: (b, j, 0)), _full((1, D)), _full(wq_bf.shape),
                  pl.BlockSpec((1, M, D), lambda b, j: (b, 0, 0)), pl.BlockSpec((1, M, D), lambda b, j: (b, 0, 0)),
                  _full(wo_bf.shape)],
        out_specs=pl.BlockSpec((1, tm, D), lambda b, j: (b, j, 0)),
        out_shape=jax.ShapeDtypeStruct((B, S, D), F32),
        compiler_params=_params("arbitrary", "arbitrary"), name="cross_attn")(x3, g.reshape(1, D), wq_bf, mk, mv, wo_bf)


def _ffn_body(x_ref, g_ref, wg_ref, wu_ref, wd_ref, gf_ref, o_ref, *, final_norm):
    x = x_ref[...]
    h = _rms(x, g_ref[...]).astype(BF16)
    gate = _dot(h, wg_ref[...])
    up = _dot(h, wu_ref[...])
    act = (jax.nn.silu(gate) * up).astype(BF16)
    y = x + _dot(act, wd_ref[...])
    if final_norm:
        y = _rms(y, gf_ref[...])
    o_ref[...] = y


def ffn(x, g, wg_bf, wu_bf, wd_bf, g_final, final_norm):
    T, D = x.shape
    tm = min(256, T)
    return pl.pallas_call(
        functools.partial(_ffn_body, final_norm=final_norm),
        grid=(T // tm,),
        in_specs=[pl.BlockSpec((tm, D), lambda i: (i, 0)), _full((1, D)), _full(wg_bf.shape), _full(wu_bf.shape),
                  _full(wd_bf.shape), _full((1, D))],
        out_specs=pl.BlockSpec((tm, D), lambda i: (i, 0)),
        out_shape=jax.ShapeDtypeStruct((T, D), F32),
        compiler_params=_params("arbitrary"), name="ffn")(x, g.reshape(1, D), wg_bf, wu_bf, wd_bf, g_final.reshape(1, D))


def _shift_rows(x, d, fill):
    row = lax.broadcasted_iota(I32, x.shape, 0)
    return jnp.where(row >= d, pltpu.roll(x, d, axis=0), fill)


def _lru_body(pc_ref, h0_ref, cv0_ref, cw_ref, cb_ref, wx_ref, bx_ref, wa_ref, ba_ref, lam_ref,
              y_out, h_out, cv_out, hc_ref, cc_ref, *, reset_first):
    j = pl.program_id(1)
    pc = pc_ref[0]
    tm = pc.shape[0]
    cw = pc.shape[1] // 2
    gate_pre = pc[:, :cw]
    x = pc[:, cw:]

    @pl.when(j == 0)
    def _():
        hc_ref[...] = h0_ref[0]
        cc_ref[...] = cv0_ref[0]

    carry = cc_ref[...]
    row8 = lax.broadcasted_iota(I32, (SUBLANES, cw), 0)
    y = cb_ref[...] + x * cw_ref[CONV_W - 1:CONV_W, :]
    for d in range(1, CONV_W):
        xs = pltpu.roll(x, d, axis=0)
        first = jnp.where(row8 < d, pltpu.roll(carry, d, axis=0), xs[:SUBLANES])
        xd = first if tm == SUBLANES else jnp.concatenate([first, xs[SUBLANES:]], axis=0)
        y = y + xd * cw_ref[CONV_W - 1 - d:CONV_W - d, :]
    new_carry = x[tm - SUBLANES:, :]
    cc_ref[...] = new_carry
    cv_out[0] = new_carry

    yb = y.astype(BF16)
    gx = jax.nn.sigmoid(_dot(yb, wx_ref[...]) + bx_ref[...])
    ga = jax.nn.sigmoid(_dot(yb, wa_ref[...]) + ba_ref[...])
    log_a = LRU_C * ga * (-_softplus(-lam_ref[...]))
    a = jnp.exp(log_a)
    th = jnp.tanh(log_a)
    mult = jnp.sqrt(-2.0 * th / (1.0 - th))
    if reset_first:
        row = lax.broadcasted_iota(I32, (tm, cw), 0)
        mult = jnp.where((row == 0) & (j == 0), 1.0, mult)
    bt = y * gx * mult
    d = 1
    while d < tm:
        bt = a * _shift_rows(bt, d, 0.0) + bt
        a = a * _shift_rows(a, d, 1.0)
        d *= 2
    h = bt + a * hc_ref[...]
    h_last = h[tm - 1:tm, :]
    hc_ref[...] = h_last
    h_out[0] = h_last
    y_out[0] = h * jax.nn.gelu(gate_pre)


def lru_mixer(pc3, h0, conv0, conv_w, conv_b, wx_bd, bx, wa_bd, ba, lam, reset_first):
    B, S, W = pc3.shape
    cw = W // 2
    tm = min(256, S)
    cv0 = jnp.concatenate([jnp.zeros((B, SUBLANES - (CONV_W - 1), cw), F32), conv0], axis=1)
    row = lambda a: a.reshape(1, cw)
    y, h_last, cv = pl.pallas_call(
        functools.partial(_lru_body, reset_first=reset_first),
        grid=(B, S // tm),
        in_specs=[pl.BlockSpec((1, tm, W), lambda b, j: (b, j, 0)), pl.BlockSpec((1, 1, cw), lambda b, j: (b, 0, 0)),
                  pl.BlockSpec((1, SUBLANES, cw), lambda b, j: (b, 0, 0)), _full((CONV_W, cw)), _full((1, cw)),
                  _full(wx_bd.shape), _full((1, cw)), _full(wa_bd.shape), _full((1, cw)), _full((1, cw))],
        out_specs=[pl.BlockSpec((1, tm, cw), lambda b, j: (b, j, 0)), pl.BlockSpec((1, 1, cw), lambda b, j: (b, 0, 0)),
                   pl.BlockSpec((1, SUBLANES, cw), lambda b, j: (b, 0, 0))],
        out_shape=[jax.ShapeDtypeStruct((B, S, cw), F32), jax.ShapeDtypeStruct((B, 1, cw), F32),
                   jax.ShapeDtypeStruct((B, SUBLANES, cw), F32)],
        scratch_shapes=[pltpu.VMEM((1, cw), F32), pltpu.VMEM((SUBLANES, cw), F32)],
        compiler_params=_params("arbitrary", "arbitrary"), name="lru_mixer")(
            pc3, h0.reshape(B, 1, cw), cv0, conv_w, row(conv_b), wx_bd, row(bx), wa_bd, row(ba), row(lam))
    return y, h_last[:, 0], cv[:, SUBLANES - (CONV_W - 1):]


def _rope_tables(pos, head_dim):
    rot = head_dim // 4
    half = rot // 2
    inv = ROPE_THETA ** (-jnp.arange(half, dtype=F32) * 2.0 / rot)
    ang = pos.astype(F32)[:, None] * inv[None, :]
    cos, sin = jnp.cos(ang), jnp.sin(ang)
    S = pos.shape[0]
    rest = head_dim - rot
    c = jnp.concatenate([cos, cos, jnp.ones((S, rest), F32)], -1)
    sa = jnp.concatenate([-sin, jnp.zeros((S, head_dim - half), F32)], -1)
    sb = jnp.concatenate([jnp.zeros((S, half), F32), sin, jnp.zeros((S, rest), F32)], -1)
    rep = LANES // head_dim
    return tuple(jnp.tile(t, (1, rep)) for t in (c, sa, sb))


def _rope(x, c, sa, sb, half):
    n = x.shape[1] // LANES
    tile = (lambda t: t) if n == 1 else (lambda t: jnp.concatenate([t] * n, axis=1))
    w = x.shape[1]
    return x * tile(c) + pltpu.roll(x, w - half, axis=1) * tile(sa) + pltpu.roll(x, half, axis=1) * tile(sb)


def _rope_all(q_ref, k_ref, qi_ref, kiw_ref, cq_ref, saq_ref, sbq_ref, ci_ref, sai_ref, sbi_ref):
    qr = _rope(q_ref[...], cq_ref[...], saq_ref[...], sbq_ref[...], HD // 8)
    kr = _rope(k_ref[...], cq_ref[...], saq_ref[...], sbq_ref[...], HD // 8)
    qir = _rope(qi_ref[...], ci_ref[...], sai_ref[...], sbi_ref[...], D_I // 8)
    kiw = kiw_ref[...]
    lane = lax.broadcasted_iota(I32, kiw.shape, 1)
    iski = lane < D_I
    kiwr = _rope(kiw, jnp.where(iski, ci_ref[...], 1.0), jnp.where(iski, sai_ref[...], 0.0),
                 jnp.where(iski, sbi_ref[...], 0.0), D_I // 8)
    return qr, kr, qir, kiwr


def _dsa_prep_sample_body(q_ref, k_ref, qi_ref, kiw_ref, cq_ref, saq_ref, sbq_ref, ci_ref, sai_ref, sbi_ref,
                          qr_out, kr_out, qir_out, kiwr_out):
    qr, kr, qir, kiwr = _rope_all(q_ref, k_ref, qi_ref, kiw_ref, cq_ref, saq_ref, sbq_ref, ci_ref, sai_ref, sbi_ref)
    qr_out[...] = qr
    kr_out[...] = kr
    qir_out[...] = qir
    kiwr_out[...] = kiwr


def _dsa_prep_prompt_body(q_ref, k_ref, v_ref, qi_ref, kiw_ref, cq_ref, saq_ref, sbq_ref, ci_ref, sai_ref, sbi_ref,
                          kr_out, kiwr_out, wq_out, wi_out, wrow_out, kbf_out, kib_out, vt_out):
    qr, kr, qir, kiwr = _rope_all(q_ref, k_ref, qi_ref, kiw_ref, cq_ref, saq_ref, sbq_ref, ci_ref, sai_ref, sbi_ref)
    tm = qr.shape[0]
    kr_out[...] = kr
    kiwr_out[...] = kiwr
    kbf_out[...] = kr.astype(BF16)
    kib_out[...] = kiwr.astype(BF16)
    vt_out[0] = v_ref[...].T.astype(BF16)
    qt = (qr * (HD ** -0.5 * LOG2E)).T
    z = jnp.zeros((HD, tm), F32)
    slabs = []
    for h in range(H_D):
        piece = qt[h * HD:(h + 1) * HD, :]
        slabs.append(jnp.concatenate([piece, z] if h < H_D // N_KV else [z, piece], axis=0))
    wq_out[0] = jnp.concatenate(slabs, axis=1).astype(BF16)
    qit = qir.T
    zi = jnp.zeros((LANES - D_I, tm), F32)
    wi_out[0] = jnp.concatenate([jnp.concatenate([qit[h * D_I:(h + 1) * D_I, :], zi], axis=0) for h in range(H_I)],
                                axis=1).astype(BF16)
    wrow_out[0] = kiwr.T[D_I:D_I + H_I, :] * (H_I ** -0.5 * D_I ** -0.5)


def dsa_prep(q, k, v, qi, kiw, tabs_q, tabs_i, prompt):
    T = q.shape[0]
    tm = Q_BLOCK if prompt else min(256, T)
    tok = lambda n: pl.BlockSpec((tm, n), lambda i: (i, 0))
    nq, nk, ni = q.shape[1], k.shape[1], qi.shape[1]
    tabs = list(tabs_q) + list(tabs_i)
    if not prompt:
        return pl.pallas_call(
            _dsa_prep_sample_body,
            grid=(T // tm,),
            in_specs=[tok(nq), tok(nk), tok(ni), tok(LANES)] + [tok(LANES)] * 6,
            out_specs=[tok(nq), tok(nk), tok(ni), tok(LANES)],
            out_shape=[jax.ShapeDtypeStruct((T, n), F32) for n in (nq, nk, ni, LANES)],
            compiler_params=_params("arbitrary"), name="dsa_prep_sample")(q, k, qi, kiw, *tabs)
    nblk = T // tm
    blk = lambda r, c: pl.BlockSpec((1, r, c), lambda i: (i, 0, 0))
    return pl.pallas_call(
        _dsa_prep_prompt_body,
        grid=(nblk,),
        in_specs=[tok(nq), tok(nk), tok(nk), tok(ni), tok(LANES)] + [tok(LANES)] * 6,
        out_specs=[tok(nk), tok(LANES), blk(nk, H_D * tm), blk(LANES, H_I * tm), blk(H_I, tm), tok(nk), tok(LANES),
                   blk(nk, tm)],
        out_shape=[jax.ShapeDtypeStruct((T, nk), F32), jax.ShapeDtypeStruct((T, LANES), F32),
                   jax.ShapeDtypeStruct((nblk, nk, H_D * tm), BF16), jax.ShapeDtypeStruct((nblk, LANES, H_I * tm), BF16),
                   jax.ShapeDtypeStruct((nblk, H_I, tm), F32), jax.ShapeDtypeStruct((T, nk), BF16),
                   jax.ShapeDtypeStruct((T, LANES), BF16), jax.ShapeDtypeStruct((nblk, nk, tm), BF16)],
        compiler_params=_params("arbitrary"), name="dsa_prep_prompt")(q, k, v, qi, kiw, *tabs)


def _key_to_f32(k):
    return pltpu.bitcast(k ^ ((k >> 31) & 0x7FFFFFFF), F32)


def _kth_largest(count_ge, shape, k):
    def step(it, key):
        cand = key + jnp.left_shift(jnp.int32(1), 31 - it)
        return jnp.where(count_ge(_key_to_f32(cand)) >= k, cand, key)

    key = lax.fori_loop(0, 32, step, jnp.full(shape, INT_MIN, I32))
    return jnp.where(key <= KEY_NEG_INF, -jnp.inf, _key_to_f32(key))


def _dsa_prompt_body(wq_ref, wi_ref, wrow_ref, kbf_ref, kib_ref, vt_ref, o_ref,
                     sc_ref, acc_ref, m_ref, l_ref, *, topk, kbs, kba):
    i = pl.program_id(1)
    qb = Q_BLOCK
    q0 = i * qb
    ls, la = kbs * qb, kba * qb
    ns = (q0 + qb + ls - 1) // ls
    na = (q0 + qb + la - 1) // la
    sub = lax.broadcasted_iota(I32, (qb, qb), 0)
    lane = lax.broadcasted_iota(I32, (qb, qb), 1)
    wi = wi_ref[0]
    wrow = wrow_ref[0]

    def score_block(kb, carry):
        k0 = kb * ls
        r = _dot(kib_ref[0, pl.ds(k0, ls), :], wi)
        for c in range(kbs):
            rc = r[c * qb:(c + 1) * qb, :]
            acc = wrow[0:1, :] * jnp.maximum(rc[:, 0:qb], 0.0)
            for h in range(1, H_I):
                acc = acc + wrow[h:h + 1, :] * jnp.maximum(rc[:, h * qb:(h + 1) * qb], 0.0)
            causal = (k0 + c * qb + sub) <= (q0 + lane)
            sc_ref[pl.ds(k0 + c * qb, qb), :] = jnp.where(causal, acc, -jnp.inf)
        return carry

    lax.fori_loop(0, ns, score_block, 0)

    def count(pred):
        def body(kb, cnt):
            for c in range(kbs):
                m = jnp.where(pred(sc_ref[pl.ds(kb * ls + c * qb, qb), :]), 1, 0)
                cnt = cnt + jnp.sum(m.reshape(qb // SUBLANES, SUBLANES, qb), axis=0)
            return cnt
        cnt = lax.fori_loop(0, ns, body, jnp.zeros((SUBLANES, qb), I32))
        return jnp.sum(cnt, axis=0, keepdims=True)

    thr = _kth_largest(lambda t: count(lambda s: s >= t), (1, qb), topk)
    need = (topk - count(lambda s: s > thr)).astype(F32)
    excess = (count(lambda s: s >= thr) > topk) & (thr > -jnp.inf)

    @pl.when(jnp.max(jnp.where(excess, 1, 0)) > 0)
    def _():
        ltri = jnp.where(lane < sub, 1.0, 0.0).astype(BF16)

        def body(c, run):
            s = sc_ref[pl.ds(c * qb, qb), :]
            eq = s == thr
            eqf = jnp.where(eq, 1.0, 0.0)
            pre = _dot(ltri, eqf.astype(BF16)) + run
            sc_ref[pl.ds(c * qb, qb), :] = jnp.where(eq & (pre >= need), -jnp.inf, s)
            return run + jnp.sum(eqf, axis=0, keepdims=True)

        lax.fori_loop(0, ns * kbs, body, jnp.zeros((1, qb), F32))

    wq = wq_ref[0]
    m_ref[...] = jnp.full(m_ref.shape, NEG, F32)
    l_ref[...] = jnp.zeros(l_ref.shape, F32)
    acc_ref[...] = jnp.zeros(acc_ref.shape, F32)
    hpg = H_D // N_KV

    def attend(kb, carry):
        logits = _dot(kbf_ref[0, pl.ds(kb * la, la), :], wq)
        s = sc_ref[pl.ds(kb * la, la), :]
        bias = jnp.where((s >= thr) & (s > -jnp.inf), 0.0, NEG)
        ps, alphas = [], []
        for h in range(H_D):
            lg = logits[:, h * qb:(h + 1) * qb] + bias
            m_old = m_ref[h:h + 1, :]
            m_new = jnp.maximum(m_old, jnp.max(lg, axis=0, keepdims=True))
            p = jnp.exp2(lg - m_new)
            alpha = jnp.exp2(m_old - m_new)
            m_ref[h:h + 1, :] = m_new
            l_ref[h:h + 1, :] = alpha * l_ref[h:h + 1, :] + jnp.sum(p, axis=0, keepdims=True)
            ps.append(p.astype(BF16))
            alphas.append(alpha)
        vt = vt_ref[0, kb]
        for g in range(N_KV):
            pg = jnp.concatenate(ps[g * hpg:(g + 1) * hpg], axis=1)
            ag = jnp.concatenate(alphas[g * hpg:(g + 1) * hpg], axis=1)
            gs = slice(g * hpg * qb, (g + 1) * hpg * qb)
            acc_ref[:, gs] = acc_ref[:, gs] * ag + _dot(vt[g * HD:(g + 1) * HD, :], pg)
        return carry

    lax.fori_loop(0, na, attend, 0)
    linv = 1.0 / l_ref[...]
    o_ref[0] = jnp.concatenate([acc_ref[:, h * qb:(h + 1) * qb] * linv[h:h + 1, :] for h in range(H_D)], axis=0).T


def dsa_prompt(wq, wi, wrow, kbf, kib, vt, B, S):
    nq = S // Q_BLOCK
    topk = min(TOPK_MAX, S // 4)
    kbs = 4 if nq % 4 == 0 else (2 if nq % 2 == 0 else 1)
    kba = min(kbs, 2)
    la = kba * Q_BLOCK
    nk = kbf.shape[-1]
    vt4 = vt.reshape(B, S // la, kba, nk, Q_BLOCK).transpose(0, 1, 3, 2, 4).reshape(B, S // la, nk, la)
    qspec = lambda r, c: pl.BlockSpec((1, r, c), lambda b, i: (b * nq + i, 0, 0))
    return pl.pallas_call(
        functools.partial(_dsa_prompt_body, topk=topk, kbs=kbs, kba=kba),
        grid=(B, nq),
        in_specs=[qspec(nk, H_D * Q_BLOCK), qspec(LANES, H_I * Q_BLOCK), qspec(H_I, Q_BLOCK),
                  pl.BlockSpec((1, S, nk), lambda b, i: (b, 0, 0)), pl.BlockSpec((1, S, LANES), lambda b, i: (b, 0, 0)),
                  pl.BlockSpec((1, S // la, nk, la), lambda b, i: (b, 0, 0, 0))],
        out_specs=pl.BlockSpec((1, Q_BLOCK, H_D * HD), lambda b, i: (b, i, 0)),
        out_shape=jax.ShapeDtypeStruct((B, S, H_D * HD), F32),
        scratch_shapes=[pltpu.VMEM((S, Q_BLOCK), F32), pltpu.VMEM((HD, H_D * Q_BLOCK), F32),
                        pltpu.VMEM((H_D, Q_BLOCK), F32), pltpu.VMEM((H_D, Q_BLOCK), F32)],
        compiler_params=_params("arbitrary", "arbitrary"), name="dsa_prompt")(
            wq, wi, wrow, kbf.reshape(B, S, nk), kib.reshape(B, S, LANES), vt4)


def _page_copies(pt_ref, pool_ref, buf_ref, sem_ref, step, slot, pps):
    return [pltpu.make_async_copy(pool_ref.at[pt_ref[step * pps + i]], buf_ref.at[slot, i], sem_ref.at[slot])
            for i in range(pps)]


def _index_scores(qi, wb, pages_bf, t):
    r = _dot_nt(qi, pages_bf)
    acc = wb[0:t, :] * jnp.maximum(r[0:t, :], 0.0)
    for h in range(1, H_I):
        acc = acc + wb[h * t:(h + 1) * t, :] * jnp.maximum(r[h * t:(h + 1) * t, :], 0.0)
    return acc


def _dsa_s_scores_body(pt_ref, qi_ref, wb_ref, new_ref, pool_ref, s_ref, snew_ref, buf_ref, sem_ref, *, pps, nc, t):
    s = pl.program_id(0)
    slot = s % 2
    copies = functools.partial(_page_copies, pt_ref, pool_ref, buf_ref, sem_ref, pps=pps)

    @pl.when(s == 0)
    def _():
        for cp in copies(s, slot):
            cp.start()

    @pl.when(s + 1 < pl.num_programs(0))
    def _():
        for cp in copies(s + 1, 1 - slot):
            cp.start()

    qi = qi_ref[0]
    wb = wb_ref[0]

    @pl.when(s % nc == 0)
    def _():
        acc = _index_scores(qi, wb, new_ref[0].astype(BF16), t)
        sub = lax.broadcasted_iota(I32, acc.shape, 0)
        lane = lax.broadcasted_iota(I32, acc.shape, 1)
        snew_ref[0] = jnp.where(lane <= sub, acc, -jnp.inf)

    for cp in copies(s, slot):
        cp.wait()
    pages = buf_ref[slot]
    for i in range(pps):
        s_ref[0, i] = _index_scores(qi, wb, pages[i].astype(BF16), t)


def _dsa_s_attend_body(pt_ref, qm_ref, sp_ref, sn_ref, knew_ref, vnew_ref, kpool_ref, vpool_ref, o_ref,
                       kbuf_ref, vbuf_ref, sem_ref, sc_ref, thr_ref, m_ref, l_ref, acc_ref, *, pps, nc, t, topk):
    s = pl.program_id(0)
    c = s % nc
    slot = s % 2
    npg = nc * pps
    kcopies = functools.partial(_page_copies, pt_ref, kpool_ref, kbuf_ref, sem_ref.at[0], pps=pps)
    vcopies = functools.partial(_page_copies, pt_ref, vpool_ref, vbuf_ref, sem_ref.at[1], pps=pps)

    @pl.when(s == 0)
    def _():
        for cp in kcopies(s, slot) + vcopies(s, slot):
            cp.start()

    @pl.when(s + 1 < pl.num_programs(0))
    def _():
        for cp in kcopies(s + 1, 1 - slot) + vcopies(s + 1, 1 - slot):
            cp.start()

    @pl.when(c == 0)
    def _():
        sc_ref[0:npg] = sp_ref[0]
        sc_ref[npg] = sn_ref[0]

        def count(pred):
            cnt = jnp.zeros((t, PAGE_SIZE), I32)
            for ch in range(npg + 1):
                cnt = cnt + jnp.where(pred(sc_ref[ch]), 1, 0)
            return jnp.sum(cnt, axis=1, keepdims=True)

        thr = _kth_largest(lambda x: count(lambda sc: sc >= x), (t, 1), topk)
        need = (topk - count(lambda sc: sc > thr)).astype(F32)
        excess = (count(lambda sc: sc >= thr) > topk) & (thr > -jnp.inf)

        @pl.when(jnp.max(jnp.where(excess, 1, 0)) > 0)
        def _():
            sub = lax.broadcasted_iota(I32, (PAGE_SIZE, PAGE_SIZE), 0)
            lane = lax.broadcasted_iota(I32, (PAGE_SIZE, PAGE_SIZE), 1)
            utri = jnp.where(sub < lane, 1.0, 0.0).astype(BF16)

            def body(ch, run):
                sc = sc_ref[ch]
                eq = sc == thr
                eqf = jnp.where(eq, 1.0, 0.0)
                pre = _dot(eqf.astype(BF16), utri) + run
                sc_ref[ch] = jnp.where(eq & (pre >= need), -jnp.inf, sc)
                return run + jnp.sum(eqf, axis=1, keepdims=True)

            lax.fori_loop(0, npg + 1, body, jnp.zeros((t, 1), F32))

        thr_ref[...] = jnp.broadcast_to(thr, thr_ref.shape)
        m_ref[...] = jnp.full(m_ref.shape, NEG, F32)
        l_ref[...] = jnp.zeros(l_ref.shape, F32)
        acc_ref[...] = jnp.zeros(acc_ref.shape, F32)

    qm = qm_ref[0]
    thr = thr_ref[...]

    def bias_rows(sc):
        b = jnp.where((sc >= thr) & (sc > -jnp.inf), 0.0, NEG)
        return jnp.concatenate([b] * H_D, axis=0)

    def update(k_bf, v_bf, bias):
        lg = _dot_nt(qm, k_bf) + bias
        m_old = m_ref[...]
        m_new = jnp.maximum(m_old, jnp.max(lg, axis=1, keepdims=True))
        p = jnp.exp2(lg - m_new)
        alpha = jnp.exp2(m_old - m_new)
        m_ref[...] = m_new
        l_ref[...] = alpha * l_ref[...] + jnp.sum(p, axis=1, keepdims=True)
        acc_ref[...] = alpha * acc_ref[...] + _dot(p.astype(BF16), v_bf)

    for cp in kcopies(s, slot) + vcopies(s, slot):
        cp.wait()
    nkv = kbuf_ref.shape[-1]
    bias = jnp.concatenate([bias_rows(sc_ref[c * pps + i]) for i in range(pps)], axis=1)
    update(kbuf_ref[slot].reshape(pps * PAGE_SIZE, nkv).astype(BF16),
           vbuf_ref[slot].reshape(pps * PAGE_SIZE, nkv).astype(BF16), bias)

    @pl.when(c == nc - 1)
    def _():
        update(knew_ref[0].astype(BF16), vnew_ref[0].astype(BF16), bias_rows(sc_ref[npg]))
        o_ref[0] = acc_ref[...] / l_ref[...]


def dsa_sample(qr, kr, v, qir, kiwr, pool_k, pool_v, pool_ki, page_table, B, T):
    npg = page_table.shape[1]
    npool = pool_k.shape[0]
    nkv = N_KV * HD
    topk = min(TOPK_MAX, (npg * PAGE_SIZE + T) // 4)
    pps = 16 if npg % 16 == 0 else npg
    nc = npg // pps
    pad_new = lambda a: jnp.pad(a.reshape(B, T, -1), ((0, 0), (0, PAGE_SIZE - T), (0, 0)))
    ki_new = pad_new(kiwr[:, :D_I])
    k_new = pad_new(kr)
    v_new = pad_new(v)
    qi_m = qir.reshape(B, T, H_I, D_I).transpose(0, 2, 1, 3).reshape(B, H_I * T, D_I).astype(BF16)
    wi = kiwr[:, D_I:D_I + H_I].reshape(B, T, H_I) * (H_I ** -0.5 * D_I ** -0.5)
    wb = jnp.broadcast_to(wi.transpose(0, 2, 1).reshape(B, H_I * T, 1), (B, H_I * T, PAGE_SIZE))
    pt_flat = page_table.reshape(-1)
    per_b = lambda r, c: pl.BlockSpec((1, r, c), lambda s, pt: (s // nc, 0, 0))
    hbm = pl.BlockSpec(memory_space=pl.ANY)
    sc_past, sc_new = pl.pallas_call(
        functools.partial(_dsa_s_scores_body, pps=pps, nc=nc, t=T),
        grid_spec=pltpu.PrefetchScalarGridSpec(
            num_scalar_prefetch=1, grid=(B * nc,),
            in_specs=[per_b(H_I * T, D_I), per_b(H_I * T, PAGE_SIZE), per_b(PAGE_SIZE, D_I), hbm],
            out_specs=[pl.BlockSpec((1, pps, T, PAGE_SIZE), lambda s, pt: (s // nc, s % nc, 0, 0)), per_b(T, PAGE_SIZE)],
            scratch_shapes=[pltpu.VMEM((2, pps, PAGE_SIZE, D_I), F32), pltpu.SemaphoreType.DMA((2,))]),
        out_shape=[jax.ShapeDtypeStruct((B, npg, T, PAGE_SIZE), F32), jax.ShapeDtypeStruct((B, T, PAGE_SIZE), F32)],
        compiler_params=_params("arbitrary"), name="dsa_sample_scores")(pt_flat, qi_m, wb, ki_new, pool_ki)
    q4 = (qr * (HD ** -0.5 * LOG2E)).reshape(B, T, N_KV, H_D // N_KV, HD).transpose(0, 2, 3, 1, 4)
    eye = jnp.eye(N_KV, dtype=F32)
    qm = (q4[:, :, :, :, None, :] * eye[None, :, None, None, :, None]).reshape(B, H_D * T, nkv).astype(BF16)
    out = pl.pallas_call(
        functools.partial(_dsa_s_attend_body, pps=pps, nc=nc, t=T, topk=topk),
        grid_spec=pltpu.PrefetchScalarGridSpec(
            num_scalar_prefetch=1, grid=(B * nc,),
            in_specs=[per_b(H_D * T, nkv), pl.BlockSpec((1, npg, T, PAGE_SIZE), lambda s, pt: (s // nc, 0, 0, 0)),
                      per_b(T, PAGE_SIZE), per_b(PAGE_SIZE, nkv), per_b(PAGE_SIZE, nkv), hbm, hbm],
            out_specs=per_b(H_D * T, nkv),
            scratch_shapes=[pltpu.VMEM((2, pps, PAGE_SIZE, nkv), F32), pltpu.VMEM((2, pps, PAGE_SIZE, nkv), F32),
                            pltpu.SemaphoreType.DMA((2, 2)), pltpu.VMEM((npg + 1, T, PAGE_SIZE), F32),
                            pltpu.VMEM((T, PAGE_SIZE), F32), pltpu.VMEM((H_D * T, 1), F32),
                            pltpu.VMEM((H_D * T, 1), F32), pltpu.VMEM((H_D * T, nkv), F32)]),
        out_shape=jax.ShapeDtypeStruct((B, H_D * T, nkv), F32),
        compiler_params=_params("arbitrary"), name="dsa_sample_attend")(
            pt_flat, qm, sc_past, sc_new, k_new, v_new, pool_k.reshape(npool, PAGE_SIZE, nkv),
            pool_v.reshape(npool, PAGE_SIZE, nkv))
    o5 = out.reshape(B, N_KV, H_D // N_KV, T, N_KV, HD)
    og = jnp.stack([o5[:, g, :, :, g, :] for g in range(N_KV)], axis=1)
    return og.transpose(0, 3, 1, 2, 4).reshape(B * T, H_D * HD)


def _block_diag(w):
    n, a, b = w.shape
    eye = jnp.eye(n, dtype=w.dtype)
    return (w[:, :, None, :] * eye[:, None, :, None]).reshape(n * a, n * b)


def _prep_weights(P):
    bf = lambda a: a.astype(BF16)
    bw = HB * NB
    pb_w = 3 * bw + W_LORA + A_LORA + G_LORA
    perm = np.concatenate([np.arange(0, bw), np.arange(bw + W_LORA, 3 * bw + W_LORA), np.arange(bw, bw + W_LORA),
                           np.arange(3 * bw + W_LORA, pb_w)])
    W = {"perm": perm, "inv_perm": np.argsort(perm)}
    aw2 = P["e_w_in"].shape[-1] - pb_w
    W["e_w_in"] = [bf(jnp.concatenate([w[:, :aw2], w[:, aw2:][:, perm]], axis=1)) for w in P["e_w_in"]]
    W["b_mu"] = [m[perm] for m in P["b_mu"]]
    W["wB_pad"] = [bf(jnp.concatenate([w, jnp.zeros((A_LORA, bw), F32)], 0)) for w in P["b_wB"]]
    W["aB_pad"] = [bf(jnp.concatenate([jnp.zeros((W_LORA, bw), F32), w], 0)) for w in P["b_aB"]]
    W["gB"] = [bf(w) for w in P["b_gB"]]
    W["ones_bd"] = bf(_block_diag(jnp.ones((HB, NB, NB), F32)))
    o_in = []
    for w in P["o_w_in"]:
        n_main = w.shape[1] - (D_I + H_I)
        o_in.append(bf(jnp.concatenate([w, jnp.zeros((w.shape[0], LANES - (D_I + H_I)), F32)], axis=1)))
        assert n_main % LANES == 0
    W["o_w_in"] = o_in
    W["wx_bd"] = [bf(_block_diag(w)) for w in P["c_wx"]]
    W["wa_bd"] = [bf(_block_diag(w)) for w in P["c_wa"]]
    for name in ("e_w_out", "o_w_out", "w_xq", "w_xo", "w_ff_gate", "w_ff_up", "w_ff_down"):
        W[name] = [bf(w) for w in P[name]]
    W["w_xkv"] = [bf(jnp.concatenate([k, v], axis=1)) for k, v in zip(P["w_xk"], P["w_xv"])]
    return W


def _memory_kv(mem, norm_mem, W):
    B, M, D = mem.shape
    ks, vs = [], []
    for layer in range(len(W["w_xkv"])):
        k, v = rms_matmul(mem.reshape(B * M, D), norm_mem[layer], W["w_xkv"][layer], (D, D))
        ks.append(k.reshape(B, M, XH, D // XH))
        vs.append(v.reshape(B, M, XH, D // XH))
    return jnp.stack(ks, 0), jnp.stack(vs, 0)


def _trunk(x3, pos0, rwkv_s0, rwkv_shift0, lru_h0, lru_conv0, mem_k, mem_v, P, W, sample_ctx):
    B, S, D = x3.shape
    T = B * S
    prompt = sample_ctx is None
    depth = P["norm_mix"].shape[0]
    x = x3.reshape(T, D)
    pos = pos0 + jnp.arange(S, dtype=I32)
    out = {k: [] for k in ("chunk_v", "rwkv_s", "rwkv_sh", "lru_h", "lru_cv", "att_k", "att_v", "att_ki")}
    bw = HB * NB
    for layer in range(depth):
        j = layer // 2
        if layer % 2 == 0:
            aw2 = W["e_w_in"][j].shape[1] - (3 * bw + W_LORA + A_LORA + G_LORA)
            pa, pb = rms_matmul(x, P["norm_mix"][layer], W["e_w_in"][j], (aw2, W["e_w_in"][j].shape[1] - aw2))
            tril = jnp.tril(P["a_ws"][j])
            if S % CHUNK == 0:
                mix, bias = tril, P["a_bs"][j]
            else:
                assert S <= CHUNK
                eye = jnp.eye(B, dtype=F32)
                mix = (tril[:, None, :S, None, :S] * eye[None, :, None, :, None]).reshape(A_G, T, T)
                bias = jnp.tile(P["a_bs"][j][:, :S], (1, B))
            ac = aw2 // 2 // A_G
            bias_full = jnp.broadcast_to(bias[:, :, None], bias.shape + (ac,))
            ya, v_rows = chunk_mixer(pa, P["a_ln_g"][j], P["a_ln_b"][j], mix.astype(BF16), bias_full)
            pbw = pb.shape[1]
            sh0 = rwkv_shift0[j][:, W["perm"]]
            r, dec, k2, v, kk, bb, g, bv, last = rwkv_prep(
                pb.reshape(B, S, pbw), sh0, W["b_mu"][j], P["b_w0"][j], W["wB_pad"][j], P["b_a0"][j], W["aB_pad"][j],
                W["gB"][j], P["b_kk"][j], P["b_ka"][j], P["b_rk"][j].reshape(-1), W["ones_bd"])
            Bg = LANES // (2 * HB)
            G = B // Bg
            grp = lambda a: a.reshape(G, Bg, S, bw)
            y_s, s_last = rwkv_scan(_to_scan(grp(r)), _to_scan(grp(dec)), _to_scan(grp(k2)), _to_scan(grp(kk)),
                                    _to_scan(grp(bb)), _v_to_scan(grp(v)),
                                    _state_to_scan(rwkv_s0[j].astype(F32).reshape(G, Bg, HB, NB, NB)))
            y = _y_from_scan(y_s, Bg).reshape(T, bw)
            x = mix0_out(ya, y, bv.reshape(T, bw), g.reshape(T, bw), x, P["b_gn_g"][j], P["b_gn_b"][j], W["ones_bd"],
                         W["e_w_out"][j])
            out["chunk_v"].append(v_rows.reshape(B, S, -1))
            out["rwkv_s"].append(_state_from_scan(s_last, Bg).reshape(B, HB, NB, NB))
            out["rwkv_sh"].append(last[:, 0][:, W["inv_perm"]])
        else:
            cw2 = 2 * P["c_conv_b"].shape[-1]
            nq, nk, ni = H_D * HD, N_KV * HD, H_I * D_I
            pc, q, k, v, qi, kiw = rms_matmul(x, P["norm_mix"][layer], W["o_w_in"][j], (cw2, nq, nk, nk, ni, LANES))
            yc, h_last, conv_new = lru_mixer(pc.reshape(B, S, cw2), lru_h0[j], lru_conv0[j], P["c_conv_w"][j],
                                             P["c_conv_b"][j], W["wx_bd"][j], P["c_bx"][j], W["wa_bd"][j], P["c_ba"][j],
                                             P["c_lambda"][j], reset_first=prompt)
            tabs_q = tuple(jnp.tile(t, (B, 1)) for t in _rope_tables(pos, HD))
            tabs_i = tuple(jnp.tile(t, (B, 1)) for t in _rope_tables(pos, D_I))
            if prompt:
                kr, kiwr, wq, wi, wrow, kbf, kib, vt = dsa_prep(q, k, v, qi, kiw, tabs_q, tabs_i, True)
                yd = dsa_prompt(wq, wi, wrow, kbf, kib, vt, B, S).reshape(T, nq)
            else:
                qr, kr, qir, kiwr = dsa_prep(q, k, v, qi, kiw, tabs_q, tabs_i, False)
                pool_k, pool_v, pool_ki, page_table = sample_ctx
                yd = dsa_sample(qr, kr, v, qir, kiwr, pool_k[j], pool_v[j], pool_ki[j], page_table, B, S)
            x = mix1_out(yc.reshape(T, -1), yd, x, W["o_w_out"][j])
            out["lru_h"].append(h_last)
            out["lru_cv"].append(conv_new)
            out["att_k"].append(kr.reshape(B, S, N_KV, HD))
            out["att_v"].append(v.reshape(B, S, N_KV, HD))
            out["att_ki"].append(kiwr[:, :D_I].reshape(B, S, D_I))
        M = mem_k.shape[2]
        x = cross_attn(x.reshape(B, S, D), P["norm_x"][layer], W["w_xq"][layer], mem_k[layer].reshape(B, M, D),
                       mem_v[layer].reshape(B, M, D), W["w_xo"][layer]).reshape(T, D)
        x = ffn(x, P["norm_ffn"][layer], W["w_ff_gate"][layer], W["w_ff_up"][layer], W["w_ff_down"][layer],
                P["norm_final"], final_norm=(layer == depth - 1))
    st = lambda l: jnp.stack(l, 0)
    return (x.reshape(B, S, D),) + tuple(st(out[k]) for k in
                                         ("chunk_v", "rwkv_s", "rwkv_sh", "lru_h", "lru_cv", "att_k", "att_v", "att_ki"))


def kernel(x_prompt, x_sample, mem_prompt, state_rwkv, state_rwkv_shift, state_lru_h, state_lru_conv, cache_attn_k, cache_attn_v, cache_attn_kidx, cache_mem_k, cache_mem_v, page_table, norm_mix, norm_x, norm_mem, norm_ffn, norm_final, w_xq, w_xk, w_xv, w_xo, w_ff_gate, w_ff_up, w_ff_down, e_w_in, e_w_out, a_ln_g, a_ln_b, a_ws, a_bs, b_mu, b_w0, b_wB, b_a0, b_aB, b_gB, b_kk, b_ka, b_rk, b_gn_g, b_gn_b, o_w_in, o_w_out, c_conv_w, c_conv_b, c_wx, c_bx, c_wa, c_ba, c_lambda):
    P = dict(norm_mix=norm_mix, norm_x=norm_x, norm_ffn=norm_ffn, norm_final=norm_final,
             w_xq=w_xq, w_xk=w_xk, w_xv=w_xv, w_xo=w_xo, w_ff_gate=w_ff_gate, w_ff_up=w_ff_up, w_ff_down=w_ff_down,
             e_w_in=e_w_in, e_w_out=e_w_out, a_ln_g=a_ln_g, a_ln_b=a_ln_b, a_ws=a_ws, a_bs=a_bs,
             b_mu=b_mu, b_w0=b_w0, b_wB=b_wB, b_a0=b_a0, b_aB=b_aB, b_gB=b_gB, b_kk=b_kk, b_ka=b_ka,
             b_rk=b_rk, b_gn_g=b_gn_g, b_gn_b=b_gn_b,
             o_w_in=o_w_in, o_w_out=o_w_out, c_conv_w=c_conv_w, c_conv_b=c_conv_b,
             c_wx=c_wx, c_bx=c_bx, c_wa=c_wa, c_ba=c_ba, c_lambda=c_lambda)
    W = _prep_weights(P)
    dt = x_prompt.dtype
    bp = x_prompt.shape[0]
    n_even, n_odd = state_rwkv.shape[0], state_lru_h.shape[0]
    pb_w = state_rwkv_shift.shape[-1]
    cw = state_lru_h.shape[-1]

    p_mem_k, p_mem_v = _memory_kv(mem_prompt, norm_mem, W)
    (y_prompt, _, p_rwkv_state, p_rwkv_shift, p_lru_h, p_lru_conv, p_attn_k, p_attn_v, p_attn_kidx) = _trunk(
        x_prompt, 0,
        jnp.zeros((n_even, bp, HB, NB, NB), dt), jnp.zeros((n_even, bp, pb_w), dt),
        jnp.zeros((n_odd, bp, cw), dt), jnp.zeros((n_odd, bp, CONV_W - 1, cw), dt),
        p_mem_k, p_mem_v, P, W, None)

    past_len = page_table.shape[1] * PAGE_SIZE
    (y_sample, s_chunk_v, s_rwkv_state, s_rwkv_shift, s_lru_h, s_lru_conv, s_attn_k, s_attn_v, s_attn_kidx) = _trunk(
        x_sample, past_len, state_rwkv, state_rwkv_shift, state_lru_h, state_lru_conv,
        cache_mem_k, cache_mem_v, P, W, (cache_attn_k, cache_attn_v, cache_attn_kidx, page_table))

    return (y_prompt, y_sample,
            p_rwkv_state, p_rwkv_shift, p_lru_h, p_lru_conv, p_attn_k, p_attn_v, p_attn_kidx, p_mem_k, p_mem_v,
            s_chunk_v, s_rwkv_state, s_rwkv_shift, s_lru_h, s_lru_conv, s_attn_k, s_attn_v, s_attn_kidx)
```

```python
import functools

import numpy as np
import jax
import jax.numpy as jnp
from jax import lax
from jax.experimental import pallas as pl
from jax.experimental.pallas import tpu as pltpu

F32 = jnp.float32
BF16 = jnp.bfloat16
I32 = jnp.int32
INT_MIN = -2 ** 31

EPS = 1e-6
GN_EPS = 64e-5
CHUNK = 128
A_G = 4
HB = 8
NB = 64
W_LORA = 64
A_LORA = 64
G_LORA = 128
NBLK = 8
CONV_W = 4
LRU_C = 8.0
H_D = 8
HD = 64
N_KV = 2
H_I = 8
D_I = 32
TOPK_MAX = 256
Q_BLOCK = 128
PAGE_SIZE = 128
ROPE_THETA = 500000.0
XH = 4
LANES = 128
SUBLANES = 8
VMEM_LIMIT = 56 * 1024 * 1024
NEG = -1e30
LOG2E = 1.4426950408889634
KEY_NEG_INF = -2139095041


def _params(*sem):
    return pltpu.CompilerParams(dimension_semantics=sem, vmem_limit_bytes=VMEM_LIMIT)


def _dot(a, b):
    return jnp.dot(a, b, preferred_element_type=F32)


def _dot_nt(a, b):
    return lax.dot_general(a, b, (((1,), (1,)), ((), ())), preferred_element_type=F32)


def _rms(x, g):
    return x * lax.rsqrt(jnp.mean(x * x, axis=-1, keepdims=True) + EPS) * g


def _segsum(x, ones_bf):
    hi = x.astype(BF16)
    lo = (x - hi.astype(F32)).astype(BF16)
    return _dot(hi, ones_bf) + _dot(lo, ones_bf)


def _softplus(x):
    return jnp.maximum(x, 0.0) + jnp.log1p(jnp.exp(-jnp.abs(x)))


def _full(shape):
    n = len(shape)
    return pl.BlockSpec(shape, lambda *_: (0,) * n)


def _rms_matmul_body(x_ref, g_ref, w_ref, *out_refs, splits):
    h = _rms(x_ref[...], g_ref[...]).astype(BF16)
    off = 0
    for o_ref, n in zip(out_refs, splits):
        o_ref[...] = _dot(h, w_ref[:, off:off + n])
        off += n


def rms_matmul(x, g, w_bf, splits):
    T, D = x.shape
    N = w_bf.shape[1]
    assert sum(splits) == N
    tm = min(256, T)
    return pl.pallas_call(
        functools.partial(_rms_matmul_body, splits=tuple(splits)),
        grid=(T // tm,),
        in_specs=[pl.BlockSpec((tm, D), lambda i: (i, 0)), _full((1, D)), _full((D, N))],
        out_specs=[pl.BlockSpec((tm, n), lambda i: (i, 0)) for n in splits],
        out_shape=[jax.ShapeDtypeStruct((T, n), F32) for n in splits],
        compiler_params=_params("arbitrary"), name="rms_matmul")(x, g.reshape(1, D), w_bf)


def _chunk_mixer_body(pa_ref, lng_ref, lnb_ref, m_ref, bias_ref, ya_ref, *maybe_v_ref, ng):
    z = jax.nn.gelu(pa_ref[...])
    aw = z.shape[1] // 2
    u = z[:, :aw]
    v = z[:, aw:]
    mu = jnp.mean(v, -1, keepdims=True)
    var = jnp.mean(jnp.square(v - mu), -1, keepdims=True)
    v = (v - mu) * lax.rsqrt(var + EPS) * lng_ref[...] + lnb_ref[...]
    for v_ref in maybe_v_ref:
        v_ref[...] = v
    vb = v.astype(BF16)
    ac = aw // ng
    for g in range(ng):
        s = _dot(m_ref[g], vb[:, g * ac:(g + 1) * ac]) + bias_ref[g]
        ya_ref[:, g * ac:(g + 1) * ac] = u[:, g * ac:(g + 1) * ac] * s


def chunk_mixer(pa, ln_g, ln_b, mix_bf, bias, emit_v):
    T, W = pa.shape
    aw = W // 2
    ng, tm, _ = mix_bf.shape
    nout = 2 if emit_v else 1
    return pl.pallas_call(
        functools.partial(_chunk_mixer_body, ng=ng),
        grid=(T // tm,),
        in_specs=[pl.BlockSpec((tm, W), lambda i: (i, 0)), _full((1, aw)), _full((1, aw)),
                  _full(mix_bf.shape), _full(bias.shape)],
        out_specs=[pl.BlockSpec((tm, aw), lambda i: (i, 0))] * nout,
        out_shape=[jax.ShapeDtypeStruct((T, aw), F32)] * nout,
        compiler_params=_params("arbitrary"), name="chunk_mixer")(pa, ln_g.reshape(1, aw), ln_b.reshape(1, aw), mix_bf, bias)


def _rwkv_prep_body(pb_ref, sh0_ref, mu_ref, w0_ref, wB_ref, a0_ref, aB_ref, gB_ref, ka_ref, rk_ref, ones_ref,
                    r_out, w_out, k_out, a_out, v_out, g_out, bv_out, last_out, carry_ref):
    j = pl.program_id(1)
    pb = pb_ref[0]
    tm = pb.shape[0]
    bw = r_out.shape[2]

    @pl.when(j == 0)
    def _():
        carry_ref[...] = sh0_ref[0]

    row = lax.broadcasted_iota(I32, pb.shape, 0)
    prev = jnp.where(row == 0, carry_ref[...], pltpu.roll(pb, 1, axis=0))
    last = pb[tm - 1:tm, :]
    carry_ref[...] = last
    last_out[0] = last
    pm = pb + (prev - pb) * mu_ref[...]
    r = pm[:, 0:bw]
    k = pm[:, bw:2 * bw]
    v = pm[:, 2 * bw:3 * bw]
    wa = pm[:, 3 * bw:3 * bw + W_LORA + A_LORA]
    gl = pm[:, 3 * bw + W_LORA + A_LORA:]
    ones = ones_ref[...]
    w = -_softplus(-(w0_ref[...] + _dot(jnp.tanh(wa).astype(BF16), wB_ref[...]))) - 0.5
    decay = jnp.exp(-jnp.exp(w))
    a = jax.nn.sigmoid(a0_ref[...] + _dot(wa.astype(BF16), aB_ref[...]))
    g = _dot(jax.nn.sigmoid(gl).astype(BF16), gB_ref[...])
    k2 = k * (1.0 + (a - 1.0) * ka_ref[...])
    bonus = _segsum(r * k2 * rk_ref[...], ones)
    r_out[0] = r
    w_out[0] = decay
    k_out[0] = k
    a_out[0] = a
    v_out[0] = v
    g_out[0] = g
    bv_out[0] = bonus * v


def rwkv_prep(pb3, shift0, mu, w0, wB_pad, a0, aB_pad, gB, k_a, r_k, ones_bf):
    B, S, PB = pb3.shape
    bw = w0.shape[-1]
    tm = min(256, S)
    row = lambda a: a.reshape(1, -1)
    tok = pl.BlockSpec((1, tm, bw), lambda b, j: (b, j, 0))
    outs = pl.pallas_call(
        _rwkv_prep_body,
        grid=(B, S // tm),
        in_specs=[pl.BlockSpec((1, tm, PB), lambda b, j: (b, j, 0)), pl.BlockSpec((1, 1, PB), lambda b, j: (b, 0, 0)),
                  _full((1, PB)), _full((1, bw)), _full(wB_pad.shape), _full((1, bw)), _full(aB_pad.shape),
                  _full(gB.shape), _full((1, bw)), _full((1, bw)), _full(ones_bf.shape)],
        out_specs=[tok] * 7 + [pl.BlockSpec((1, 1, PB), lambda b, j: (b, 0, 0))],
        out_shape=[jax.ShapeDtypeStruct((B, S, bw), F32)] * 7 + [jax.ShapeDtypeStruct((B, 1, PB), F32)],
        scratch_shapes=[pltpu.VMEM((1, PB), F32)],
        compiler_params=_params("arbitrary", "arbitrary"), name="rwkv_prep")(
            pb3, shift0.reshape(B, 1, PB), row(mu), row(w0), wB_pad, row(a0), aB_pad, gB, row(k_a), row(r_k), ones_bf)
    return outs


def _rwkv_scan_body(r_ref, w_ref, k_ref, a_ref, v_ref, s0_ref, kkp_ref, kap_ref, y_ref, sl_ref, S_ref, op_ref, *, tc, nip):
    c = pl.program_id(1)

    @pl.when(c == 0)
    def _():
        S_ref[...] = s0_ref[0]

    dup = lambda x: jnp.concatenate([x, x], axis=1)

    def derive(t, slot):
        k = dup(k_ref[0, t])
        a = dup(a_ref[0, t])
        kk = k * kkp_ref[...]
        kk = kk * lax.rsqrt(jnp.sum(kk * kk, axis=0, keepdims=True) + 1e-12)
        op_ref[slot, 0] = dup(r_ref[0, t])
        op_ref[slot, 1] = dup(w_ref[0, t])
        op_ref[slot, 2] = k * (1.0 + (a - 1.0) * kap_ref[...])
        op_ref[slot, 3] = kk
        op_ref[slot, 4] = kk * a

    def advance(t, slot):
        for ip in range(nip):
            Sg = S_ref[ip]
            sa = jnp.sum(Sg * op_ref[slot, 3], axis=0, keepdims=True)
            vrow = v_ref[0, t, pl.ds(ip, 1), :]
            Sg = Sg * op_ref[slot, 1] - sa * op_ref[slot, 4] + vrow * op_ref[slot, 2]
            S_ref[ip] = Sg
            y_ref[0, t, pl.ds(ip, 1), :] = jnp.sum(Sg * op_ref[slot, 0], axis=0, keepdims=True)

    derive(0, 0)

    def step_pair(u, carry):
        t0 = 2 * u
        derive(t0 + 1, 1)
        advance(t0, 0)
        derive(jnp.minimum(t0 + 2, tc - 1), 0)
        advance(t0 + 1, 1)
        return carry

    assert tc % 2 == 0
    lax.fori_loop(0, tc // 2, step_pair, 0)

    @pl.when(c == pl.num_programs(1) - 1)
    def _():
        sl_ref[0] = S_ref[...]


def rwkv_scan(r, w, k, a, v, s0, kkp, kap):
    G, S, nb, hl = r.shape
    nip, L = v.shape[2:]
    tc = min(64, S)
    vec = pl.BlockSpec((1, tc, nb, hl), lambda g, c: (g, c, 0, 0))
    vsp = pl.BlockSpec((1, tc, nip, L), lambda g, c: (g, c, 0, 0))
    ssp = pl.BlockSpec((1, nip, nb, L), lambda g, c: (g, 0, 0, 0))
    return pl.pallas_call(
        functools.partial(_rwkv_scan_body, tc=tc, nip=nip),
        grid=(G, S // tc),
        in_specs=[vec] * 4 + [vsp, ssp, _full((nb, L)), _full((nb, L))],
        out_specs=[vsp, ssp],
        out_shape=[jax.ShapeDtypeStruct((G, S, nip, L), F32), jax.ShapeDtypeStruct((G, nip, nb, L), F32)],
        scratch_shapes=[pltpu.VMEM((nip, nb, L), F32), pltpu.VMEM((2, 5, nb, L), F32)],
        compiler_params=_params("arbitrary", "arbitrary"), name="rwkv_scan")(r, w, k, a, v, s0, kkp, kap)


def _to_scan(x):
    G, Bg, S, _ = x.shape
    return x.reshape(G, Bg, S, HB, NB).transpose(0, 2, 4, 1, 3).reshape(G, S, NB, Bg * HB)


def _param_to_scan(p, Bg):
    return jnp.tile(p.reshape(HB, NB).T, (1, 2 * Bg))


def _v_to_scan(x):
    G, Bg, S, _ = x.shape
    return x.reshape(G, Bg, S, HB, 2, NB // 2).transpose(0, 2, 5, 4, 1, 3).reshape(G, S, NB // 2, 2 * Bg * HB)


def _y_from_scan(y, Bg):
    G, S = y.shape[:2]
    return y.reshape(G, S, NB // 2, 2, Bg, HB).transpose(0, 4, 1, 5, 3, 2).reshape(G, Bg, S, HB * NB)


def _state_to_scan(s):
    G, Bg = s.shape[:2]
    return s.reshape(G, Bg, HB, 2, NB // 2, NB).transpose(0, 4, 5, 3, 1, 2).reshape(G, NB // 2, NB, 2 * Bg * HB)


def _state_from_scan(s, Bg):
    G = s.shape[0]
    return s.reshape(G, NB // 2, NB, 2, Bg, HB).transpose(0, 4, 5, 3, 1, 2).reshape(G, Bg, HB, NB, NB)


def _mix0_out_body(ya_ref, y_ref, bv_ref, g_ref, x_ref, gng_ref, gnb_ref, ones_ref, w_ref, o_ref):
    ones = ones_ref[...]
    y = y_ref[...]
    mean = _segsum(y, ones) * (1.0 / NB)
    d = y - mean
    var = _segsum(d * d, ones) * (1.0 / NB)
    yb = (d * lax.rsqrt(var + GN_EPS) * gng_ref[...] + gnb_ref[...] + bv_ref[...]) * g_ref[...]
    aw = ya_ref.shape[1]
    o_ref[...] = x_ref[...] + _dot(ya_ref[...].astype(BF16), w_ref[:aw, :]) + _dot(yb.astype(BF16), w_ref[aw:, :])


def mix0_out(ya, y, bv, g, x, gn_g, gn_b, ones_bf, w_bf):
    T, D = x.shape
    aw = ya.shape[1]
    bw = y.shape[1]
    tm = min(256, T)
    tok = lambda n: pl.BlockSpec((tm, n), lambda i: (i, 0))
    return pl.pallas_call(
        _mix0_out_body,
        grid=(T // tm,),
        in_specs=[tok(aw), tok(bw), tok(bw), tok(bw), tok(D), _full((1, bw)), _full((1, bw)), _full(ones_bf.shape),
                  _full(w_bf.shape)],
        out_specs=tok(D),
        out_shape=jax.ShapeDtypeStruct((T, D), F32),
        compiler_params=_params("arbitrary"), name="mix0_out")(ya, y, bv, g, x, gn_g.reshape(1, bw), gn_b.reshape(1, bw),
                                                              ones_bf, w_bf)


def _mix1_out_body(a1_ref, a2_ref, x_ref, w_ref, o_ref):
    n1 = a1_ref.shape[1]
    o_ref[...] = x_ref[...] + _dot(a1_ref[...].astype(BF16), w_ref[:n1, :]) + _dot(a2_ref[...].astype(BF16), w_ref[n1:, :])


def mix1_out(a1, a2, x, w_bf):
    T, D = x.shape
    tm = min(256, T)
    tok = lambda n: pl.BlockSpec((tm, n), lambda i: (i, 0))
    return pl.pallas_call(
        _mix1_out_body,
        grid=(T // tm,),
        in_specs=[tok(a1.shape[1]), tok(a2.shape[1]), tok(D), _full(w_bf.shape)],
        out_specs=tok(D),
        out_shape=jax.ShapeDtypeStruct((T, D), F32),
        compiler_params=_params("arbitrary"), name="mix1_out")(a1, a2, x, w_bf)


def _xattn_body(x_ref, g_ref, wq_ref, mk_ref, mv_ref, wo_ref, o_ref, *, nh):
    x = x_ref[0]
    h = _rms(x, g_ref[...]).astype(BF16)
    q = _dot(h, wq_ref[...])
    mk = mk_ref[0].astype(BF16)
    mv = mv_ref[0].astype(BF16)
    hd = q.shape[1] // nh
    outs = []
    for hh in range(nh):
        sl = slice(hh * hd, (hh + 1) * hd)
        lg = _dot_nt(q[:, sl].astype(BF16), mk[:, sl]) * (hd ** -0.5)
        e = jnp.exp(lg - jnp.max(lg, -1, keepdims=True))
        p = e / jnp.sum(e, -1, keepdims=True)
        outs.append(_dot(p.astype(BF16), mv[:, sl]))
    o = jnp.concatenate(outs, axis=1).astype(BF16)
    o_ref[0] = x + _dot(o, wo_ref[...])


def cross_attn(x3, g, wq_bf, mk, mv, wo_bf):
    B, S, D = x3.shape
    M = mk.shape[1]
    tm = min(256, S)
    return pl.pallas_call(
        functools.partial(_xattn_body, nh=XH),
        grid=(B, S // tm),
        in_specs=[pl.BlockSpec((1, tm, D), lambda b, j: (b, j, 0)), _full((1, D)), _full(wq_bf.shape),
                  pl.BlockSpec((1, M, D), lambda b, j: (b, 0, 0)), pl.BlockSpec((1, M, D), lambda b, j: (b, 0, 0)),
                  _full(wo_bf.shape)],
        out_specs=pl.BlockSpec((1, tm, D), lambda b, j: (b, j, 0)),
        out_shape=jax.ShapeDtypeStruct((B, S, D), F32),
        compiler_params=_params("arbitrary", "arbitrary"), name="cross_attn")(x3, g.reshape(1, D), wq_bf, mk, mv, wo_bf)


def _ffn_body(x_ref, g_ref, wg_ref, wu_ref, wd_ref, gf_ref, o_ref, *, final_norm):
    x = x_ref[...]
    h = _rms(x, g_ref[...]).astype(BF16)
    gate = _dot(h, wg_ref[...])
    up = _dot(h, wu_ref[...])
    act = (jax.nn.silu(gate) * up).astype(BF16)
    y = x + _dot(act, wd_ref[...])
    if final_norm:
        y = _rms(y, gf_ref[...])
    o_ref[...] = y


def ffn(x, g, wg_bf, wu_bf, wd_bf, g_final, final_norm):
    T, D = x.shape
    tm = min(256, T)
    return pl.pallas_call(
        functools.partial(_ffn_body, final_norm=final_norm),
        grid=(T // tm,),
        in_specs=[pl.BlockSpec((tm, D), lambda i: (i, 0)), _full((1, D)), _full(wg_bf.shape), _full(wu_bf.shape),
                  _full(wd_bf.shape), _full((1, D))],
        out_specs=pl.BlockSpec((tm, D), lambda i: (i, 0)),
        out_shape=jax.ShapeDtypeStruct((T, D), F32),
        compiler_params=_params("arbitrary"), name="ffn")(x, g.reshape(1, D), wg_bf, wu_bf, wd_bf, g_final.reshape(1, D))


def _shift_rows(x, d, fill):
    row = lax.broadcasted_iota(I32, x.shape, 0)
    return jnp.where(row >= d, pltpu.roll(x, d, axis=0), fill)


def _lru_body(pc_ref, h0_ref, cv0_ref, cw_ref, cb_ref, wx_ref, bx_ref, wa_ref, ba_ref, lam_ref,
              y_out, h_out, cv_out, hc_ref, cc_ref, *, reset_first):
    j = pl.program_id(1)
    pc = pc_ref[0]
    tm = pc.shape[0]
    cw = pc.shape[1] // 2
    gate_pre = pc[:, :cw]
    x = pc[:, cw:]

    @pl.when(j == 0)
    def _():
        hc_ref[...] = h0_ref[0]
        cc_ref[...] = cv0_ref[0]

    carry = cc_ref[...]
    row8 = lax.broadcasted_iota(I32, (SUBLANES, cw), 0)
    y = cb_ref[...] + x * cw_ref[CONV_W - 1:CONV_W, :]
    for d in range(1, CONV_W):
        xs = pltpu.roll(x, d, axis=0)
        first = jnp.where(row8 < d, pltpu.roll(carry, d, axis=0), xs[:SUBLANES])
        xd = first if tm == SUBLANES else jnp.concatenate([first, xs[SUBLANES:]], axis=0)
        y = y + xd * cw_ref[CONV_W - 1 - d:CONV_W - d, :]
    new_carry = x[tm - SUBLANES:, :]
    cc_ref[...] = new_carry
    cv_out[0] = new_carry

    yb = y.astype(BF16)
    gx = jax.nn.sigmoid(_dot(yb, wx_ref[...]) + bx_ref[...])
    ga = jax.nn.sigmoid(_dot(yb, wa_ref[...]) + ba_ref[...])
    log_a = LRU_C * ga * (-_softplus(-lam_ref[...]))
    a = jnp.exp(log_a)
    th = jnp.tanh(log_a)
    mult = jnp.sqrt(-2.0 * th / (1.0 - th))
    if reset_first:
        row = lax.broadcasted_iota(I32, (tm, cw), 0)
        mult = jnp.where((row == 0) & (j == 0), 1.0, mult)
    bt = y * gx * mult
    d = 1
    while d < tm:
        bt = a * _shift_rows(bt, d, 0.0) + bt
        a = a * _shift_rows(a, d, 1.0)
        d *= 2
    h = bt + a * hc_ref[...]
    h_last = h[tm - 1:tm, :]
    hc_ref[...] = h_last
    h_out[0] = h_last
    y_out[0] = h * jax.nn.gelu(gate_pre)


def lru_mixer(pc3, h0, conv0, conv_w, conv_b, wx_bd, bx, wa_bd, ba, lam, reset_first):
    B, S, W = pc3.shape
    cw = W // 2
    tm = min(256, S)
    cv0 = jnp.concatenate([jnp.zeros((B, SUBLANES - (CONV_W - 1), cw), F32), conv0], axis=1)
    row = lambda a: a.reshape(1, cw)
    y, h_last, cv = pl.pallas_call(
        functools.partial(_lru_body, reset_first=reset_first),
        grid=(B, S // tm),
        in_specs=[pl.BlockSpec((1, tm, W), lambda b, j: (b, j, 0)), pl.BlockSpec((1, 1, cw), lambda b, j: (b, 0, 0)),
                  pl.BlockSpec((1, SUBLANES, cw), lambda b, j: (b, 0, 0)), _full((CONV_W, cw)), _full((1, cw)),
                  _full(wx_bd.shape), _full((1, cw)), _full(wa_bd.shape), _full((1, cw)), _full((1, cw))],
        out_specs=[pl.BlockSpec((1, tm, cw), lambda b, j: (b, j, 0)), pl.BlockSpec((1, 1, cw), lambda b, j: (b, 0, 0)),
                   pl.BlockSpec((1, SUBLANES, cw), lambda b, j: (b, 0, 0))],
        out_shape=[jax.ShapeDtypeStruct((B, S, cw), F32), jax.ShapeDtypeStruct((B, 1, cw), F32),
                   jax.ShapeDtypeStruct((B, SUBLANES, cw), F32)],
        scratch_shapes=[pltpu.VMEM((1, cw), F32), pltpu.VMEM((SUBLANES, cw), F32)],
        compiler_params=_params("arbitrary", "arbitrary"), name="lru_mixer")(
            pc3, h0.reshape(B, 1, cw), cv0, conv_w, row(conv_b), wx_bd, row(bx), wa_bd, row(ba), row(lam))
    return y, h_last[:, 0], cv[:, SUBLANES - (CONV_W - 1):]


def _rope_tables(pos, head_dim):
    rot = head_dim // 4
    half = rot // 2
    inv = ROPE_THETA ** (-jnp.arange(half, dtype=F32) * 2.0 / rot)
    ang = pos.astype(F32)[:, None] * inv[None, :]
    cos, sin = jnp.cos(ang), jnp.sin(ang)
    S = pos.shape[0]
    rest = head_dim - rot
    c = jnp.concatenate([cos, cos, jnp.ones((S, rest), F32)], -1)
    sa = jnp.concatenate([-sin, jnp.zeros((S, head_dim - half), F32)], -1)
    sb = jnp.concatenate([jnp.zeros((S, half), F32), sin, jnp.zeros((S, rest), F32)], -1)
    rep = LANES // head_dim
    return tuple(jnp.tile(t, (1, rep)) for t in (c, sa, sb))


def _rope(x, c, sa, sb, half):
    n = x.shape[1] // LANES
    tile = (lambda t: t) if n == 1 else (lambda t: jnp.concatenate([t] * n, axis=1))
    w = x.shape[1]
    return x * tile(c) + pltpu.roll(x, w - half, axis=1) * tile(sa) + pltpu.roll(x, half, axis=1) * tile(sb)


def _rope_all(q_ref, k_ref, qi_ref, kiw_ref, cq_ref, saq_ref, sbq_ref, ci_ref, sai_ref, sbi_ref):
    qr = _rope(q_ref[...], cq_ref[...], saq_ref[...], sbq_ref[...], HD // 8)
    kr = _rope(k_ref[...], cq_ref[...], saq_ref[...], sbq_ref[...], HD // 8)
    qir = _rope(qi_ref[...], ci_ref[...], sai_ref[...], sbi_ref[...], D_I // 8)
    kiw = kiw_ref[...]
    lane = lax.broadcasted_iota(I32, kiw.shape, 1)
    iski = lane < D_I
    kiwr = _rope(kiw, jnp.where(iski, ci_ref[...], 1.0), jnp.where(iski, sai_ref[...], 0.0),
                 jnp.where(iski, sbi_ref[...], 0.0), D_I // 8)
    return qr, kr, qir, kiwr


def _dsa_prep_sample_body(q_ref, k_ref, qi_ref, kiw_ref, cq_ref, saq_ref, sbq_ref, ci_ref, sai_ref, sbi_ref,
                          qr_out, kr_out, qir_out, kiwr_out):
    qr, kr, qir, kiwr = _rope_all(q_ref, k_ref, qi_ref, kiw_ref, cq_ref, saq_ref, sbq_ref, ci_ref, sai_ref, sbi_ref)
    qr_out[...] = qr
    kr_out[...] = kr
    qir_out[...] = qir
    kiwr_out[...] = kiwr


def _dsa_prep_prompt_body(q_ref, k_ref, v_ref, qi_ref, kiw_ref, cq_ref, saq_ref, sbq_ref, ci_ref, sai_ref, sbi_ref,
                          kr_out, kiwr_out, wq_out, wi_out, wrow_out, kbf_out, kib_out, vt_out):
    qr, kr, qir, kiwr = _rope_all(q_ref, k_ref, qi_ref, kiw_ref, cq_ref, saq_ref, sbq_ref, ci_ref, sai_ref, sbi_ref)
    tm = qr.shape[0]
    kr_out[...] = kr
    kiwr_out[...] = kiwr
    kbf_out[...] = kr.astype(BF16)
    kib_out[...] = kiwr.astype(BF16)
    vt_out[0] = v_ref[...].T.astype(BF16)
    qt = (qr * (HD ** -0.5 * LOG2E)).T
    z = jnp.zeros((HD, tm), F32)
    slabs = []
    for h in range(H_D):
        piece = qt[h * HD:(h + 1) * HD, :]
        slabs.append(jnp.concatenate([piece, z] if h < H_D // N_KV else [z, piece], axis=0))
    wq_out[0] = jnp.concatenate(slabs, axis=1).astype(BF16)
    qit = qir.T
    zi = jnp.zeros((LANES - D_I, tm), F32)
    wi_out[0] = jnp.concatenate([jnp.concatenate([qit[h * D_I:(h + 1) * D_I, :], zi], axis=0) for h in range(H_I)],
                                axis=1).astype(BF16)
    wrow_out[0] = kiwr.T[D_I:D_I + H_I, :] * (H_I ** -0.5 * D_I ** -0.5)


def dsa_prep(q, k, v, qi, kiw, tabs_q, tabs_i, prompt):
    T = q.shape[0]
    tm = Q_BLOCK if prompt else min(256, T)
    tok = lambda n: pl.BlockSpec((tm, n), lambda i: (i, 0))
    nq, nk, ni = q.shape[1], k.shape[1], qi.shape[1]
    tabs = list(tabs_q) + list(tabs_i)
    if not prompt:
        return pl.pallas_call(
            _dsa_prep_sample_body,
            grid=(T // tm,),
            in_specs=[tok(nq), tok(nk), tok(ni), tok(LANES)] + [tok(LANES)] * 6,
            out_specs=[tok(nq), tok(nk), tok(ni), tok(LANES)],
            out_shape=[jax.ShapeDtypeStruct((T, n), F32) for n in (nq, nk, ni, LANES)],
            compiler_params=_params("arbitrary"), name="dsa_prep_sample")(q, k, qi, kiw, *tabs)
    nblk = T // tm
    blk = lambda r, c: pl.BlockSpec((1, r, c), lambda i: (i, 0, 0))
    nper = tabs[0].shape[0] // tm
    return pl.pallas_call(
        _dsa_prep_prompt_body,
        grid=(nblk,),
        in_specs=[tok(nq), tok(nk), tok(nk), tok(ni), tok(LANES)] + [pl.BlockSpec((tm, LANES), lambda i: (i % nper, 0))] * 6,
        out_specs=[tok(nk), tok(LANES), blk(nk, H_D * tm), blk(LANES, H_I * tm), blk(H_I, tm), tok(nk), tok(LANES),
                   blk(nk, tm)],
        out_shape=[jax.ShapeDtypeStruct((T, nk), F32), jax.ShapeDtypeStruct((T, LANES), F32),
                   jax.ShapeDtypeStruct((nblk, nk, H_D * tm), BF16), jax.ShapeDtypeStruct((nblk, LANES, H_I * tm), BF16),
                   jax.ShapeDtypeStruct((nblk, H_I, tm), F32), jax.ShapeDtypeStruct((T, nk), BF16),
                   jax.ShapeDtypeStruct((T, LANES), BF16), jax.ShapeDtypeStruct((nblk, nk, tm), BF16)],
        compiler_params=_params("arbitrary"), name="dsa_prep_prompt")(q, k, v, qi, kiw, *tabs)


def _key_to_f32(k):
    return pltpu.bitcast(k ^ ((k >> 31) & 0x7FFFFFFF), F32)


def _kth_largest(count_ge, shape, k):
    def step(it, key):
        cand = key + jnp.left_shift(jnp.int32(1), 31 - it)
        return jnp.where(count_ge(_key_to_f32(cand)) >= k, cand, key)

    key = lax.fori_loop(0, 32, step, jnp.full(shape, INT_MIN, I32))
    return jnp.where(key <= KEY_NEG_INF, -jnp.inf, _key_to_f32(key))


def _dsa_prompt_body(wq_ref, wi_ref, wrow_ref, kbf_ref, kib_ref, vt_ref, o_ref,
                     sc_ref, acc_ref, m_ref, l_ref, *bufs, topk, kbs):
    i = pl.program_id(1)
    qb = Q_BLOCK
    q0 = i * qb
    ls = kbs * qb
    la = ls // 2
    ns = (q0 + qb + ls - 1) // ls
    sub = lax.broadcasted_iota(I32, (qb, qb), 0)
    lane = lax.broadcasted_iota(I32, (qb, qb), 1)
    wi = wi_ref[0]
    wrow = wrow_ref[0]

    def score_block(kb, carry):
        k0 = kb * ls
        r = _dot(kib_ref[0, pl.ds(k0, ls), :], wi)
        for c in range(kbs):
            rc = r[c * qb:(c + 1) * qb, :]
            acc = wrow[0:1, :] * jnp.maximum(rc[:, 0:qb], 0.0)
            for h in range(1, H_I):
                acc = acc + wrow[h:h + 1, :] * jnp.maximum(rc[:, h * qb:(h + 1) * qb], 0.0)
            causal = (k0 + c * qb + sub) <= (q0 + lane)
            sc_ref[pl.ds(k0 + c * qb, qb), :] = jnp.where(causal, acc, -jnp.inf)
        return carry

    lax.fori_loop(0, ns, score_block, 0)

    def count(pred):
        def body(kb, cnt):
            for c in range(kbs):
                m = jnp.where(pred(sc_ref[pl.ds(kb * ls + c * qb, qb), :]), 1, 0)
                cnt = cnt + jnp.sum(m.reshape(qb // SUBLANES, SUBLANES, qb), axis=0)
            return cnt
        cnt = lax.fori_loop(0, ns, body, jnp.zeros((SUBLANES, qb), I32))
        return jnp.sum(cnt, axis=0, keepdims=True)

    thr = _kth_largest(lambda t: count(lambda s: s >= t), (1, qb), topk)
    need = (topk - count(lambda s: s > thr)).astype(F32)
    excess = (count(lambda s: s >= thr) > topk) & (thr > -jnp.inf)

    @pl.when(jnp.max(jnp.where(excess, 1, 0)) > 0)
    def _():
        ltri = jnp.where(lane < sub, 1.0, 0.0).astype(BF16)

        def body(c, run):
            s = sc_ref[pl.ds(c * qb, qb), :]
            eq = s == thr
            eqf = jnp.where(eq, 1.0, 0.0)
            pre = _dot(ltri, eqf.astype(BF16)) + run
            sc_ref[pl.ds(c * qb, qb), :] = jnp.where(eq & (pre >= need), -jnp.inf, s)
            return run + jnp.sum(eqf, axis=0, keepdims=True)

        lax.fori_loop(0, ns * kbs, body, jnp.zeros((1, qb), F32))

    wq = wq_ref[0]
    m_ref[...] = jnp.full(m_ref.shape, NEG, F32)
    l_ref[...] = jnp.zeros(l_ref.shape, F32)
    acc_ref[...] = jnp.zeros(acc_ref.shape, F32)
    hpg = H_D // N_KV
    (lg_a, p_a, al_a), (lg_b, p_b, al_b) = bufs[0:3], bufs[3:6]

    def logits_into(lg_ref, blk):
        lg_ref[...] = _dot(kbf_ref[0, pl.ds(blk * la, la), :], wq)

    def softmax_block(lg_ref, p_ref, al_ref, blk):
        s = sc_ref[pl.ds(blk * la, la), :]
        bias = jnp.where((s >= thr) & (s > -jnp.inf), 0.0, NEG)
        for h in range(H_D):
            hs = slice(h * qb, (h + 1) * qb)
            lg = lg_ref[:, hs] + bias
            m_old = m_ref[h:h + 1, :]
            m_new = jnp.maximum(m_old, jnp.max(lg, axis=0, keepdims=True))
            p = jnp.exp2(lg - m_new)
            alpha = jnp.exp2(m_old - m_new)
            m_ref[h:h + 1, :] = m_new
            l_ref[h:h + 1, :] = alpha * l_ref[h:h + 1, :] + jnp.sum(p, axis=0, keepdims=True)
            p_ref[:, hs] = p.astype(BF16)
            al_ref[h:h + 1, :] = alpha

    def pv_block(p_ref, al_ref, blk):
        vt = vt_ref[0, blk]
        for g in range(N_KV):
            ag = jnp.concatenate([al_ref[h:h + 1, :] for h in range(g * hpg, (g + 1) * hpg)], axis=1)
            gs = slice(g * hpg * qb, (g + 1) * hpg * qb)
            acc_ref[:, gs] = acc_ref[:, gs] * ag + _dot(vt[g * HD:(g + 1) * HD, :], p_ref[:, gs])

    last = 2 * ns - 1
    logits_into(lg_a, 0)
    p_b[...] = jnp.zeros(p_b.shape, BF16)
    al_b[...] = jnp.ones(al_b.shape, F32)

    def attend(mb, carry):
        b0 = 2 * mb
        logits_into(lg_b, b0 + 1)
        softmax_block(lg_a, p_a, al_a, b0)
        pv_block(p_b, al_b, jnp.maximum(b0 - 1, 0))
        logits_into(lg_a, jnp.minimum(b0 + 2, last))
        softmax_block(lg_b, p_b, al_b, b0 + 1)
        pv_block(p_a, al_a, b0)
        return carry

    lax.fori_loop(0, ns, attend, 0)
    pv_block(p_b, al_b, last)
    linv = 1.0 / l_ref[...]
    o_ref[0] = jnp.concatenate([acc_ref[:, h * qb:(h + 1) * qb] * linv[h:h + 1, :] for h in range(H_D)], axis=0).T


def dsa_prompt(wq, wi, wrow, kbf, kib, vt, B, S):
    nq = S // Q_BLOCK
    topk = min(TOPK_MAX, S // 4)
    assert nq % 2 == 0
    kbs = 4 if nq % 4 == 0 else 2
    kba = kbs // 2
    la = kba * Q_BLOCK
    nk = kbf.shape[-1]
    vt4 = vt.reshape(B, S // la, kba, nk, Q_BLOCK).transpose(0, 1, 3, 2, 4).reshape(B, S // la, nk, la)
    qspec = lambda r, c: pl.BlockSpec((1, r, c), lambda b, i: (b * nq + i, 0, 0))
    buf = [pltpu.VMEM((la, H_D * Q_BLOCK), F32), pltpu.VMEM((la, H_D * Q_BLOCK), BF16), pltpu.VMEM((H_D, Q_BLOCK), F32)]
    return pl.pallas_call(
        functools.partial(_dsa_prompt_body, topk=topk, kbs=kbs),
        grid=(B, nq),
        in_specs=[qspec(nk, H_D * Q_BLOCK), qspec(LANES, H_I * Q_BLOCK), qspec(H_I, Q_BLOCK),
                  pl.BlockSpec((1, S, nk), lambda b, i: (b, 0, 0)), pl.BlockSpec((1, S, LANES), lambda b, i: (b, 0, 0)),
                  pl.BlockSpec((1, S // la, nk, la), lambda b, i: (b, 0, 0, 0))],
        out_specs=pl.BlockSpec((1, Q_BLOCK, H_D * HD), lambda b, i: (b, i, 0)),
        out_shape=jax.ShapeDtypeStruct((B, S, H_D * HD), F32),
        scratch_shapes=[pltpu.VMEM((S, Q_BLOCK), F32), pltpu.VMEM((HD, H_D * Q_BLOCK), F32),
                        pltpu.VMEM((H_D, Q_BLOCK), F32), pltpu.VMEM((H_D, Q_BLOCK), F32)] + buf + buf,
        compiler_params=_params("arbitrary", "arbitrary"), name="dsa_prompt")(
            wq, wi, wrow, kbf.reshape(B, S, nk), kib.reshape(B, S, LANES), vt4)


def _page_copies(pt_ref, pool_ref, buf_ref, sem_ref, step, slot, pps):
    return [pltpu.make_async_copy(pool_ref.at[pt_ref[step * pps + i]], buf_ref.at[slot, i], sem_ref.at[slot])
            for i in range(pps)]


def _index_scores(qi, wb, pages_bf, t):
    r = _dot_nt(qi, pages_bf)
    acc = wb[0:t, :] * jnp.maximum(r[0:t, :], 0.0)
    for h in range(1, H_I):
        acc = acc + wb[h * t:(h + 1) * t, :] * jnp.maximum(r[h * t:(h + 1) * t, :], 0.0)
    return acc


def _dsa_s_scores_body(pt_ref, qi_ref, wb_ref, new_ref, pool_ref, s_ref, snew_ref, buf_ref, sem_ref, *, pps, nc, t):
    s = pl.program_id(0)
    slot = s % 2
    copies = functools.partial(_page_copies, pt_ref, pool_ref, buf_ref, sem_ref, pps=pps)

    @pl.when(s == 0)
    def _():
        for cp in copies(s, slot):
            cp.start()

    @pl.when(s + 1 < pl.num_programs(0))
    def _():
        for cp in copies(s + 1, 1 - slot):
            cp.start()

    qi = qi_ref[0]
    wb = wb_ref[0]

    @pl.when(s % nc == 0)
    def _():
        acc = _index_scores(qi, wb, new_ref[0].astype(BF16), t)
        sub = lax.broadcasted_iota(I32, acc.shape, 0)
        lane = lax.broadcasted_iota(I32, acc.shape, 1)
        snew_ref[0] = jnp.where(lane <= sub, acc, -jnp.inf)

    for cp in copies(s, slot):
        cp.wait()
    pages = buf_ref[slot]
    for i in range(pps):
        s_ref[0, i] = _index_scores(qi, wb, pages[i].astype(BF16), t)


def _dsa_s_attend_body(pt_ref, qm_ref, sp_ref, sn_ref, knew_ref, vnew_ref, kpool_ref, vpool_ref, o_ref,
                       kbuf_ref, vbuf_ref, sem_ref, sc_ref, thr_ref, m_ref, l_ref, acc_ref, *, pps, nc, t, topk):
    s = pl.program_id(0)
    c = s % nc
    slot = s % 2
    npg = nc * pps
    kcopies = functools.partial(_page_copies, pt_ref, kpool_ref, kbuf_ref, sem_ref.at[0], pps=pps)
    vcopies = functools.partial(_page_copies, pt_ref, vpool_ref, vbuf_ref, sem_ref.at[1], pps=pps)

    @pl.when(s == 0)
    def _():
        for cp in kcopies(s, slot) + vcopies(s, slot):
            cp.start()

    @pl.when(s + 1 < pl.num_programs(0))
    def _():
        for cp in kcopies(s + 1, 1 - slot) + vcopies(s + 1, 1 - slot):
            cp.start()

    @pl.when(c == 0)
    def _():
        sc_ref[0:npg] = sp_ref[0]
        sc_ref[npg] = sn_ref[0]

        def count(pred):
            cnt = jnp.zeros((t, PAGE_SIZE), I32)
            for ch in range(npg + 1):
                cnt = cnt + jnp.where(pred(sc_ref[ch]), 1, 0)
            return jnp.sum(cnt, axis=1, keepdims=True)

        thr = _kth_largest(lambda x: count(lambda sc: sc >= x), (t, 1), topk)
        need = (topk - count(lambda sc: sc > thr)).astype(F32)
        excess = (count(lambda sc: sc >= thr) > topk) & (thr > -jnp.inf)

        @pl.when(jnp.max(jnp.where(excess, 1, 0)) > 0)
        def _():
            sub = lax.broadcasted_iota(I32, (PAGE_SIZE, PAGE_SIZE), 0)
            lane = lax.broadcasted_iota(I32, (PAGE_SIZE, PAGE_SIZE), 1)
            utri = jnp.where(sub < lane, 1.0, 0.0).astype(BF16)

            def body(ch, run):
                sc = sc_ref[ch]
                eq = sc == thr
                eqf = jnp.where(eq, 1.0, 0.0)
                pre = _dot(eqf.astype(BF16), utri) + run
                sc_ref[ch] = jnp.where(eq & (pre >= need), -jnp.inf, sc)
                return run + jnp.sum(eqf, axis=1, keepdims=True)

            lax.fori_loop(0, npg + 1, body, jnp.zeros((t, 1), F32))

        thr_ref[...] = jnp.broadcast_to(thr, thr_ref.shape)
        m_ref[...] = jnp.full(m_ref.shape, NEG, F32)
        l_ref[...] = jnp.zeros(l_ref.shape, F32)
        acc_ref[...] = jnp.zeros(acc_ref.shape, F32)

    qm = qm_ref[0]
    thr = thr_ref[...]

    def bias_rows(sc):
        b = jnp.where((sc >= thr) & (sc > -jnp.inf), 0.0, NEG)
        return jnp.concatenate([b] * H_D, axis=0)

    def update(k_bf, v_bf, bias):
        lg = _dot_nt(qm, k_bf) + bias
        m_old = m_ref[...]
        m_new = jnp.maximum(m_old, jnp.max(lg, axis=1, keepdims=True))
        p = jnp.exp2(lg - m_new)
        alpha = jnp.exp2(m_old - m_new)
        m_ref[...] = m_new
        l_ref[...] = alpha * l_ref[...] + jnp.sum(p, axis=1, keepdims=True)
        acc_ref[...] = alpha * acc_ref[...] + _dot(p.astype(BF16), v_bf)

    for cp in kcopies(s, slot) + vcopies(s, slot):
        cp.wait()
    nkv = kbuf_ref.shape[-1]
    bias = jnp.concatenate([bias_rows(sc_ref[c * pps + i]) for i in range(pps)], axis=1)
    update(kbuf_ref[slot].reshape(pps * PAGE_SIZE, nkv).astype(BF16),
           vbuf_ref[slot].reshape(pps * PAGE_SIZE, nkv).astype(BF16), bias)

    @pl.when(c == nc - 1)
    def _():
        update(knew_ref[0].astype(BF16), vnew_ref[0].astype(BF16), bias_rows(sc_ref[npg]))
        o_ref[0] = acc_ref[...] / l_ref[...]


def dsa_sample(qr, kr, v, qir, kiwr, pool_k, pool_v, pool_ki, page_table, B, T):
    npg = page_table.shape[1]
    npool = pool_k.shape[0]
    nkv = N_KV * HD
    topk = min(TOPK_MAX, (npg * PAGE_SIZE + T) // 4)
    pps = 16 if npg % 16 == 0 else npg
    nc = npg // pps
    pad_new = lambda a: jnp.pad(a.reshape(B, T, -1), ((0, 0), (0, PAGE_SIZE - T), (0, 0)))
    ki_new = pad_new(kiwr[:, :D_I])
    k_new = pad_new(kr)
    v_new = pad_new(v)
    qi_m = qir.reshape(B, T, H_I, D_I).transpose(0, 2, 1, 3).reshape(B, H_I * T, D_I).astype(BF16)
    wi = kiwr[:, D_I:D_I + H_I].reshape(B, T, H_I) * (H_I ** -0.5 * D_I ** -0.5)
    wb = jnp.broadcast_to(wi.transpose(0, 2, 1).reshape(B, H_I * T, 1), (B, H_I * T, PAGE_SIZE))
    pt_flat = page_table.reshape(-1)
    per_b = lambda r, c: pl.BlockSpec((1, r, c), lambda s, pt: (s // nc, 0, 0))
    hbm = pl.BlockSpec(memory_space=pl.ANY)
    sc_past, sc_new = pl.pallas_call(
        functools.partial(_dsa_s_scores_body, pps=pps, nc=nc, t=T),
        grid_spec=pltpu.PrefetchScalarGridSpec(
            num_scalar_prefetch=1, grid=(B * nc,),
            in_specs=[per_b(H_I * T, D_I), per_b(H_I * T, PAGE_SIZE), per_b(PAGE_SIZE, D_I), hbm],
            out_specs=[pl.BlockSpec((1, pps, T, PAGE_SIZE), lambda s, pt: (s // nc, s % nc, 0, 0)), per_b(T, PAGE_SIZE)],
            scratch_shapes=[pltpu.VMEM((2, pps, PAGE_SIZE, D_I), F32), pltpu.SemaphoreType.DMA((2,))]),
        out_shape=[jax.ShapeDtypeStruct((B, npg, T, PAGE_SIZE), F32), jax.ShapeDtypeStruct((B, T, PAGE_SIZE), F32)],
        compiler_params=_params("arbitrary"), name="dsa_sample_scores")(pt_flat, qi_m, wb, ki_new, pool_ki)
    q4 = (qr * (HD ** -0.5 * LOG2E)).reshape(B, T, N_KV, H_D // N_KV, HD).transpose(0, 2, 3, 1, 4)
    eye = jnp.eye(N_KV, dtype=F32)
    qm = (q4[:, :, :, :, None, :] * eye[None, :, None, None, :, None]).reshape(B, H_D * T, nkv).astype(BF16)
    out = pl.pallas_call(
        functools.partial(_dsa_s_attend_body, pps=pps, nc=nc, t=T, topk=topk),
        grid_spec=pltpu.PrefetchScalarGridSpec(
            num_scalar_prefetch=1, grid=(B * nc,),
            in_specs=[per_b(H_D * T, nkv), pl.BlockSpec((1, npg, T, PAGE_SIZE), lambda s, pt: (s // nc, 0, 0, 0)),
                      per_b(T, PAGE_SIZE), per_b(PAGE_SIZE, nkv), per_b(PAGE_SIZE, nkv), hbm, hbm],
            out_specs=per_b(H_D * T, nkv),
            scratch_shapes=[pltpu.VMEM((2, pps, PAGE_SIZE, nkv), F32), pltpu.VMEM((2, pps, PAGE_SIZE, nkv), F32),
                            pltpu.SemaphoreType.DMA((2, 2)), pltpu.VMEM((npg + 1, T, PAGE_SIZE), F32),
                            pltpu.VMEM((T, PAGE_SIZE), F32), pltpu.VMEM((H_D * T, 1), F32),
                            pltpu.VMEM((H_D * T, 1), F32), pltpu.VMEM((H_D * T, nkv), F32)]),
        out_shape=jax.ShapeDtypeStruct((B, H_D * T, nkv), F32),
        compiler_params=_params("arbitrary"), name="dsa_sample_attend")(
            pt_flat, qm, sc_past, sc_new, k_new, v_new, pool_k.reshape(npool, PAGE_SIZE, nkv),
            pool_v.reshape(npool, PAGE_SIZE, nkv))
    o5 = out.reshape(B, N_KV, H_D // N_KV, T, N_KV, HD)
    og = jnp.stack([o5[:, g, :, :, g, :] for g in range(N_KV)], axis=1)
    return og.transpose(0, 3, 1, 2, 4).reshape(B * T, H_D * HD)


def _block_diag(w):
    n, a, b = w.shape
    eye = jnp.eye(n, dtype=w.dtype)
    return (w[:, :, None, :] * eye[:, None, :, None]).reshape(n * a, n * b)


def _prep_weights(P):
    bf = lambda a: a.astype(BF16)
    bw = HB * NB
    pb_w = 3 * bw + W_LORA + A_LORA + G_LORA
    perm = np.concatenate([np.arange(0, bw), np.arange(bw + W_LORA, 3 * bw + W_LORA), np.arange(bw, bw + W_LORA),
                           np.arange(3 * bw + W_LORA, pb_w)])
    W = {"perm": perm, "inv_perm": np.argsort(perm)}
    aw2 = P["e_w_in"].shape[-1] - pb_w
    W["e_w_in"] = [bf(jnp.concatenate([w[:, :aw2], w[:, aw2:][:, perm]], axis=1)) for w in P["e_w_in"]]
    W["b_mu"] = [m[perm] for m in P["b_mu"]]
    W["wB_pad"] = [bf(jnp.concatenate([w, jnp.zeros((A_LORA, bw), F32)], 0)) for w in P["b_wB"]]
    W["aB_pad"] = [bf(jnp.concatenate([jnp.zeros((W_LORA, bw), F32), w], 0)) for w in P["b_aB"]]
    W["gB"] = [bf(w) for w in P["b_gB"]]
    W["ones_bd"] = bf(_block_diag(jnp.ones((HB, NB, NB), F32)))
    o_in = []
    for w in P["o_w_in"]:
        n_main = w.shape[1] - (D_I + H_I)
        o_in.append(bf(jnp.concatenate([w, jnp.zeros((w.shape[0], LANES - (D_I + H_I)), F32)], axis=1)))
        assert n_main % LANES == 0
    W["o_w_in"] = o_in
    W["wx_bd"] = [bf(_block_diag(w)) for w in P["c_wx"]]
    W["wa_bd"] = [bf(_block_diag(w)) for w in P["c_wa"]]
    for name in ("e_w_out", "o_w_out", "w_xq", "w_xo", "w_ff_gate", "w_ff_up", "w_ff_down"):
        W[name] = [bf(w) for w in P[name]]
    W["w_xkv"] = [bf(jnp.concatenate([k, v], axis=1)) for k, v in zip(P["w_xk"], P["w_xv"])]
    return W


def _memory_kv(mem, norm_mem, W):
    B, M, D = mem.shape
    ks, vs = [], []
    for layer in range(len(W["w_xkv"])):
        k, v = rms_matmul(mem.reshape(B * M, D), norm_mem[layer], W["w_xkv"][layer], (D, D))
        ks.append(k.reshape(B, M, XH, D // XH))
        vs.append(v.reshape(B, M, XH, D // XH))
    return jnp.stack(ks, 0), jnp.stack(vs, 0)


def _trunk(x3, pos0, rwkv_s0, rwkv_shift0, lru_h0, lru_conv0, mem_k, mem_v, P, W, sample_ctx):
    B, S, D = x3.shape
    T = B * S
    prompt = sample_ctx is None
    depth = P["norm_mix"].shape[0]
    x = x3.reshape(T, D)
    pos = pos0 + jnp.arange(S, dtype=I32)
    out = {k: [] for k in ("chunk_v", "rwkv_s", "rwkv_sh", "lru_h", "lru_cv", "att_k", "att_v", "att_ki")}
    bw = HB * NB
    for layer in range(depth):
        j = layer // 2
        if layer % 2 == 0:
            aw2 = W["e_w_in"][j].shape[1] - (3 * bw + W_LORA + A_LORA + G_LORA)
            pa, pb = rms_matmul(x, P["norm_mix"][layer], W["e_w_in"][j], (aw2, W["e_w_in"][j].shape[1] - aw2))
            tril = jnp.tril(P["a_ws"][j])
            if S % CHUNK == 0:
                mix, bias = tril, P["a_bs"][j]
            else:
                assert S <= CHUNK
                eye = jnp.eye(B, dtype=F32)
                mix = (tril[:, None, :S, None, :S] * eye[None, :, None, :, None]).reshape(A_G, T, T)
                bias = jnp.tile(P["a_bs"][j][:, :S], (1, B))
            ac = aw2 // 2 // A_G
            bias_full = jnp.broadcast_to(bias[:, :, None], bias.shape + (ac,))
            mixed = chunk_mixer(pa, P["a_ln_g"][j], P["a_ln_b"][j], mix.astype(BF16), bias_full, emit_v=not prompt)
            ya = mixed[0]
            pbw = pb.shape[1]
            sh0 = rwkv_shift0[j][:, W["perm"]]
            r, dec, k, a, v, g, bv, last = rwkv_prep(
                pb.reshape(B, S, pbw), sh0, W["b_mu"][j], P["b_w0"][j], W["wB_pad"][j], P["b_a0"][j], W["aB_pad"][j],
                W["gB"][j], P["b_ka"][j], P["b_rk"][j].reshape(-1), W["ones_bd"])
            Bg = LANES // (2 * HB)
            G = B // Bg
            grp = lambda t: t.reshape(G, Bg, S, bw)
            y_s, s_last = rwkv_scan(_to_scan(grp(r)), _to_scan(grp(dec)), _to_scan(grp(k)), _to_scan(grp(a)),
                                    _v_to_scan(grp(v)), _state_to_scan(rwkv_s0[j].astype(F32).reshape(G, Bg, HB, NB, NB)),
                                    _param_to_scan(P["b_kk"][j], Bg), _param_to_scan(P["b_ka"][j], Bg))
            y = _y_from_scan(y_s, Bg).reshape(T, bw)
            x = mix0_out(ya, y, bv.reshape(T, bw), g.reshape(T, bw), x, P["b_gn_g"][j], P["b_gn_b"][j], W["ones_bd"],
                         W["e_w_out"][j])
            out["chunk_v"].append(mixed[-1].reshape(B, S, -1))
            out["rwkv_s"].append(_state_from_scan(s_last, Bg).reshape(B, HB, NB, NB))
            out["rwkv_sh"].append(last[:, 0][:, W["inv_perm"]])
        else:
            cw2 = 2 * P["c_conv_b"].shape[-1]
            nq, nk, ni = H_D * HD, N_KV * HD, H_I * D_I
            pc, q, k, v, qi, kiw = rms_matmul(x, P["norm_mix"][layer], W["o_w_in"][j], (cw2, nq, nk, nk, ni, LANES))
            yc, h_last, conv_new = lru_mixer(pc.reshape(B, S, cw2), lru_h0[j], lru_conv0[j], P["c_conv_w"][j],
                                             P["c_conv_b"][j], W["wx_bd"][j], P["c_bx"][j], W["wa_bd"][j], P["c_ba"][j],
                                             P["c_lambda"][j], reset_first=prompt)
            tabs_q, tabs_i = _rope_tables(pos, HD), _rope_tables(pos, D_I)
            if not prompt:
                tabs_q, tabs_i = (tuple(jnp.tile(t, (B, 1)) for t in tabs) for tabs in (tabs_q, tabs_i))
            if prompt:
                kr, kiwr, wq, wi, wrow, kbf, kib, vt = dsa_prep(q, k, v, qi, kiw, tabs_q, tabs_i, True)
                yd = dsa_prompt(wq, wi, wrow, kbf, kib, vt, B, S).reshape(T, nq)
            else:
                qr, kr, qir, kiwr = dsa_prep(q, k, v, qi, kiw, tabs_q, tabs_i, False)
                pool_k, pool_v, pool_ki, page_table = sample_ctx
                yd = dsa_sample(qr, kr, v, qir, kiwr, pool_k[j], pool_v[j], pool_ki[j], page_table, B, S)
            x = mix1_out(yc.reshape(T, -1), yd, x, W["o_w_out"][j])
            out["lru_h"].append(h_last)
            out["lru_cv"].append(conv_new)
            out["att_k"].append(kr.reshape(B, S, N_KV, HD))
            out["att_v"].append(v.reshape(B, S, N_KV, HD))
            out["att_ki"].append(kiwr[:, :D_I].reshape(B, S, D_I))
        M = mem_k.shape[2]
        x = cross_attn(x.reshape(B, S, D), P["norm_x"][layer], W["w_xq"][layer], mem_k[layer].reshape(B, M, D),
                       mem_v[layer].reshape(B, M, D), W["w_xo"][layer]).reshape(T, D)
        x = ffn(x, P["norm_ffn"][layer], W["w_ff_gate"][layer], W["w_ff_up"][layer], W["w_ff_down"][layer],
                P["norm_final"], final_norm=(layer == depth - 1))
    st = lambda l: jnp.stack(l, 0)
    return (x.reshape(B, S, D),) + tuple(st(out[k]) for k in
                                         ("chunk_v", "rwkv_s", "rwkv_sh", "lru_h", "lru_cv", "att_k", "att_v", "att_ki"))


def kernel(x_prompt, x_sample, mem_prompt, state_rwkv, state_rwkv_shift, state_lru_h, state_lru_conv, cache_attn_k, cache_attn_v, cache_attn_kidx, cache_mem_k, cache_mem_v, page_table, norm_mix, norm_x, norm_mem, norm_ffn, norm_final, w_xq, w_xk, w_xv, w_xo, w_ff_gate, w_ff_up, w_ff_down, e_w_in, e_w_out, a_ln_g, a_ln_b, a_ws, a_bs, b_mu, b_w0, b_wB, b_a0, b_aB, b_gB, b_kk, b_ka, b_rk, b_gn_g, b_gn_b, o_w_in, o_w_out, c_conv_w, c_conv_b, c_wx, c_bx, c_wa, c_ba, c_lambda):
    P = dict(norm_mix=norm_mix, norm_x=norm_x, norm_ffn=norm_ffn, norm_final=norm_final,
             w_xq=w_xq, w_xk=w_xk, w_xv=w_xv, w_xo=w_xo, w_ff_gate=w_ff_gate, w_ff_up=w_ff_up, w_ff_down=w_ff_down,
             e_w_in=e_w_in, e_w_out=e_w_out, a_ln_g=a_ln_g, a_ln_b=a_ln_b, a_ws=a_ws, a_bs=a_bs,
             b_mu=b_mu, b_w0=b_w0, b_wB=b_wB, b_a0=b_a0, b_aB=b_aB, b_gB=b_gB, b_kk=b_kk, b_ka=b_ka,
             b_rk=b_rk, b_gn_g=b_gn_g, b_gn_b=b_gn_b,
             o_w_in=o_w_in, o_w_out=o_w_out, c_conv_w=c_conv_w, c_conv_b=c_conv_b,
             c_wx=c_wx, c_bx=c_bx, c_wa=c_wa, c_ba=c_ba, c_lambda=c_lambda)
    W = _prep_weights(P)
    dt = x_prompt.dtype
    bp = x_prompt.shape[0]
    n_even, n_odd = state_rwkv.shape[0], state_lru_h.shape[0]
    pb_w = state_rwkv_shift.shape[-1]
    cw = state_lru_h.shape[-1]

    p_mem_k, p_mem_v = _memory_kv(mem_prompt, norm_mem, W)
    (y_prompt, _, p_rwkv_state, p_rwkv_shift, p_lru_h, p_lru_conv, p_attn_k, p_attn_v, p_attn_kidx) = _trunk(
        x_prompt, 0,
        jnp.zeros((n_even, bp, HB, NB, NB), dt), jnp.zeros((n_even, bp, pb_w), dt),
        jnp.zeros((n_odd, bp, cw), dt), jnp.zeros((n_odd, bp, CONV_W - 1, cw), dt),
        p_mem_k, p_mem_v, P, W, None)

    past_len = page_table.shape[1] * PAGE_SIZE
    (y_sample, s_chunk_v, s_rwkv_state, s_rwkv_shift, s_lru_h, s_lru_conv, s_attn_k, s_attn_v, s_attn_kidx) = _trunk(
        x_sample, past_len, state_rwkv, state_rwkv_shift, state_lru_h, state_lru_conv,
        cache_mem_k, cache_mem_v, P, W, (cache_attn_k, cache_attn_v, cache_attn_kidx, page_table))

    return (y_prompt, y_sample,
            p_rwkv_state, p_rwkv_shift, p_lru_h, p_lru_conv, p_attn_k, p_attn_v, p_attn_kidx, p_mem_k, p_mem_v,
            s_chunk_v, s_rwkv_state, s_rwkv_shift, s_lru_h, s_lru_conv, s_attn_k, s_attn_v, s_attn_kidx)
```

```python
import functools

import numpy as np
import jax
import jax.numpy as jnp
from jax import lax
from jax.experimental import pallas as pl
from jax.experimental.pallas import tpu as pltpu

F32 = jnp.float32
BF16 = jnp.bfloat16
I32 = jnp.int32
INT_MIN = -2 ** 31

EPS = 1e-6
GN_EPS = 64e-5
CHUNK = 128
A_G = 4
HB = 8
NB = 64
W_LORA = 64
A_LORA = 64
G_LORA = 128
NBLK = 8
CONV_W = 4
LRU_C = 8.0
H_D = 8
HD = 64
N_KV = 2
H_I = 8
D_I = 32
TOPK_MAX = 256
Q_BLOCK = 128
PAGE_SIZE = 128
ROPE_THETA = 500000.0
XH = 4
LANES = 128
SUBLANES = 8
VMEM_LIMIT = 56 * 1024 * 1024
NEG = -1e30
LOG2E = 1.4426950408889634
KEY_NEG_INF = -2139095041


def _params(*sem):
    return pltpu.CompilerParams(dimension_semantics=sem, vmem_limit_bytes=VMEM_LIMIT)


def _dot(a, b):
    return jnp.dot(a, b, preferred_element_type=F32)


def _dot_nt(a, b):
    return lax.dot_general(a, b, (((1,), (1,)), ((), ())), preferred_element_type=F32)


def _rms(x, g):
    return x * lax.rsqrt(jnp.mean(x * x, axis=-1, keepdims=True) + EPS) * g


def _segsum(x, ones_bf):
    hi = x.astype(BF16)
    lo = (x - hi.astype(F32)).astype(BF16)
    return _dot(hi, ones_bf) + _dot(lo, ones_bf)


def _softplus(x):
    return jnp.maximum(x, 0.0) + jnp.log1p(jnp.exp(-jnp.abs(x)))


def _full(shape):
    n = len(shape)
    return pl.BlockSpec(shape, lambda *_: (0,) * n)


def _rms_matmul_body(x_ref, g_ref, w_ref, *out_refs, splits):
    h = _rms(x_ref[...], g_ref[...]).astype(BF16)
    off = 0
    for o_ref, n in zip(out_refs, splits):
        o_ref[...] = _dot(h, w_ref[:, off:off + n])
        off += n


def rms_matmul(x, g, w_bf, splits):
    T, D = x.shape
    N = w_bf.shape[1]
    assert sum(splits) == N
    tm = min(256, T)
    return pl.pallas_call(
        functools.partial(_rms_matmul_body, splits=tuple(splits)),
        grid=(T // tm,),
        in_specs=[pl.BlockSpec((tm, D), lambda i: (i, 0)), _full((1, D)), _full((D, N))],
        out_specs=[pl.BlockSpec((tm, n), lambda i: (i, 0)) for n in splits],
        out_shape=[jax.ShapeDtypeStruct((T, n), F32) for n in splits],
        compiler_params=_params("arbitrary"), name="rms_matmul")(x, g.reshape(1, D), w_bf)


def _chunk_mixer_body(pa_ref, lng_ref, lnb_ref, m_ref, bias_ref, ya_ref, *maybe_v_ref, ng):
    z = jax.nn.gelu(pa_ref[...])
    aw = z.shape[1] // 2
    u = z[:, :aw]
    v = z[:, aw:]
    mu = jnp.mean(v, -1, keepdims=True)
    var = jnp.mean(jnp.square(v - mu), -1, keepdims=True)
    v = (v - mu) * lax.rsqrt(var + EPS) * lng_ref[...] + lnb_ref[...]
    for v_ref in maybe_v_ref:
        v_ref[...] = v
    vb = v.astype(BF16)
    ac = aw // ng
    for g in range(ng):
        s = _dot(m_ref[g], vb[:, g * ac:(g + 1) * ac]) + bias_ref[g]
        ya_ref[:, g * ac:(g + 1) * ac] = u[:, g * ac:(g + 1) * ac] * s


def chunk_mixer(pa, ln_g, ln_b, mix_bf, bias, emit_v):
    T, W = pa.shape
    aw = W // 2
    ng, tm, _ = mix_bf.shape
    nout = 2 if emit_v else 1
    return pl.pallas_call(
        functools.partial(_chunk_mixer_body, ng=ng),
        grid=(T // tm,),
        in_specs=[pl.BlockSpec((tm, W), lambda i: (i, 0)), _full((1, aw)), _full((1, aw)),
                  _full(mix_bf.shape), _full(bias.shape)],
        out_specs=[pl.BlockSpec((tm, aw), lambda i: (i, 0))] * nout,
        out_shape=[jax.ShapeDtypeStruct((T, aw), F32)] * nout,
        compiler_params=_params("arbitrary"), name="chunk_mixer")(pa, ln_g.reshape(1, aw), ln_b.reshape(1, aw), mix_bf, bias)


def _rwkv_prep_body(pb_ref, sh0_ref, mu_ref, w0_ref, wB_ref, a0_ref, aB_ref, gB_ref, ka_ref, rk_ref, ones_ref,
                    r_out, w_out, k_out, a_out, v_out, g_out, bv_out, last_out, carry_ref):
    j = pl.program_id(1)
    pb = pb_ref[0]
    tm = pb.shape[0]
    bw = r_out.shape[2]

    @pl.when(j == 0)
    def _():
        carry_ref[...] = sh0_ref[0]

    row = lax.broadcasted_iota(I32, pb.shape, 0)
    prev = jnp.where(row == 0, carry_ref[...], pltpu.roll(pb, 1, axis=0))
    last = pb[tm - 1:tm, :]
    carry_ref[...] = last
    last_out[0] = last
    pm = pb + (prev - pb) * mu_ref[...]
    r = pm[:, 0:bw]
    k = pm[:, bw:2 * bw]
    v = pm[:, 2 * bw:3 * bw]
    wa = pm[:, 3 * bw:3 * bw + W_LORA + A_LORA]
    gl = pm[:, 3 * bw + W_LORA + A_LORA:]
    ones = ones_ref[...]
    w = -_softplus(-(w0_ref[...] + _dot(jnp.tanh(wa).astype(BF16), wB_ref[...]))) - 0.5
    decay = jnp.exp(-jnp.exp(w))
    a = jax.nn.sigmoid(a0_ref[...] + _dot(wa.astype(BF16), aB_ref[...]))
    g = _dot(jax.nn.sigmoid(gl).astype(BF16), gB_ref[...])
    k2 = k * (1.0 + (a - 1.0) * ka_ref[...])
    bonus = _segsum(r * k2 * rk_ref[...], ones)
    r_out[0] = r
    w_out[0] = decay
    k_out[0] = k
    a_out[0] = a
    v_out[0] = v
    g_out[0] = g
    bv_out[0] = bonus * v


def rwkv_prep(pb3, shift0, mu, w0, wB_pad, a0, aB_pad, gB, k_a, r_k, ones_bf):
    B, S, PB = pb3.shape
    bw = w0.shape[-1]
    tm = min(256, S)
    row = lambda a: a.reshape(1, -1)
    tok = pl.BlockSpec((1, tm, bw), lambda b, j: (b, j, 0))
    outs = pl.pallas_call(
        _rwkv_prep_body,
        grid=(B, S // tm),
        in_specs=[pl.BlockSpec((1, tm, PB), lambda b, j: (b, j, 0)), pl.BlockSpec((1, 1, PB), lambda b, j: (b, 0, 0)),
                  _full((1, PB)), _full((1, bw)), _full(wB_pad.shape), _full((1, bw)), _full(aB_pad.shape),
                  _full(gB.shape), _full((1, bw)), _full((1, bw)), _full(ones_bf.shape)],
        out_specs=[tok] * 7 + [pl.BlockSpec((1, 1, PB), lambda b, j: (b, 0, 0))],
        out_shape=[jax.ShapeDtypeStruct((B, S, bw), F32)] * 7 + [jax.ShapeDtypeStruct((B, 1, PB), F32)],
        scratch_shapes=[pltpu.VMEM((1, PB), F32)],
        compiler_params=_params("arbitrary", "arbitrary"), name="rwkv_prep")(
            pb3, shift0.reshape(B, 1, PB), row(mu), row(w0), wB_pad, row(a0), aB_pad, gB, row(k_a), row(r_k), ones_bf)
    return outs


def _rwkv_scan_body(r_ref, w_ref, k_ref, a_ref, v_ref, s0_ref, kkp_ref, kap_ref, y_ref, sl_ref,
                    S_ref, op_ref, vs_ref, ys_ref, *, tc, nip):
    c = pl.program_id(1)
    hl = r_ref.shape[-1]

    @pl.when(c == 0)
    def _():
        S_ref[...] = s0_ref[0]

    dup = lambda x: jnp.concatenate([x, x], axis=1)

    def derive(t, slot):
        k = dup(k_ref[0, t])
        a = dup(a_ref[0, t])
        kk = k * kkp_ref[...]
        kk = kk * lax.rsqrt(jnp.sum(kk * kk, axis=0, keepdims=True) + 1e-12)
        op_ref[slot, 0] = dup(r_ref[0, t])
        op_ref[slot, 1] = dup(w_ref[0, t])
        op_ref[slot, 2] = k * (1.0 + (a - 1.0) * kap_ref[...])
        op_ref[slot, 3] = kk
        op_ref[slot, 4] = kk * a
        v = v_ref[0, t]
        vs_ref[slot] = jnp.concatenate([v[:nip, :], v[nip:, :]], axis=1)

    def advance(t, slot):
        for ip in range(nip):
            Sg = S_ref[ip]
            sa = jnp.sum(Sg * op_ref[slot, 3], axis=0, keepdims=True)
            vrow = vs_ref[slot, pl.ds(ip, 1), :]
            Sg = Sg * op_ref[slot, 1] - sa * op_ref[slot, 4] + vrow * op_ref[slot, 2]
            S_ref[ip] = Sg
            ys_ref[slot, pl.ds(ip, 1), :] = jnp.sum(Sg * op_ref[slot, 0], axis=0, keepdims=True)

    def flush(t, slot):
        y = ys_ref[slot]
        y_ref[0, t] = jnp.concatenate([y[:, :hl], y[:, hl:]], axis=0)

    derive(0, 0)
    ys_ref[1] = jnp.zeros(ys_ref.shape[1:], F32)

    def step_pair(u, carry):
        t0 = 2 * u
        derive(t0 + 1, 1)
        flush(jnp.maximum(t0 - 1, 0), 1)
        advance(t0, 0)
        derive(jnp.minimum(t0 + 2, tc - 1), 0)
        flush(t0, 0)
        advance(t0 + 1, 1)
        return carry

    assert tc % 2 == 0
    lax.fori_loop(0, tc // 2, step_pair, 0)
    flush(tc - 1, 1)

    @pl.when(c == pl.num_programs(1) - 1)
    def _():
        sl_ref[0] = S_ref[...]


def rwkv_scan(r, w, k, a, v, s0, kkp, kap):
    G, S, nb, hl = r.shape
    nip, L = nb // 2, 2 * hl
    tc = min(64, S)
    vec = pl.BlockSpec((1, tc, nb, hl), lambda g, c: (g, c, 0, 0))
    ssp = pl.BlockSpec((1, nip, nb, L), lambda g, c: (g, 0, 0, 0))
    return pl.pallas_call(
        functools.partial(_rwkv_scan_body, tc=tc, nip=nip),
        grid=(G, S // tc),
        in_specs=[vec] * 5 + [ssp, _full((nb, L)), _full((nb, L))],
        out_specs=[vec, ssp],
        out_shape=[jax.ShapeDtypeStruct((G, S, nb, hl), F32), jax.ShapeDtypeStruct((G, nip, nb, L), F32)],
        scratch_shapes=[pltpu.VMEM((nip, nb, L), F32), pltpu.VMEM((2, 5, nb, L), F32), pltpu.VMEM((2, nip, L), F32),
                        pltpu.VMEM((2, nip, L), F32)],
        compiler_params=_params("arbitrary", "arbitrary"), name="rwkv_scan")(r, w, k, a, v, s0, kkp, kap)


def _to_scan(x):
    G, Bg, S, _ = x.shape
    return x.reshape(G, Bg, S, HB, NB).transpose(0, 2, 4, 1, 3).reshape(G, S, NB, Bg * HB)


def _param_to_scan(p, Bg):
    return jnp.tile(p.reshape(HB, NB).T, (1, 2 * Bg))


def _from_scan(y, Bg):
    G, S = y.shape[:2]
    return y.reshape(G, S, NB, Bg, HB).transpose(0, 3, 1, 4, 2).reshape(G, Bg, S, HB * NB)


def _state_to_scan(s):
    G, Bg = s.shape[:2]
    return s.reshape(G, Bg, HB, 2, NB // 2, NB).transpose(0, 4, 5, 3, 1, 2).reshape(G, NB // 2, NB, 2 * Bg * HB)


def _state_from_scan(s, Bg):
    G = s.shape[0]
    return s.reshape(G, NB // 2, NB, 2, Bg, HB).transpose(0, 4, 5, 3, 1, 2).reshape(G, Bg, HB, NB, NB)


def _mix0_out_body(ya_ref, y_ref, bv_ref, g_ref, x_ref, gng_ref, gnb_ref, ones_ref, w_ref, o_ref):
    ones = ones_ref[...]
    y = y_ref[...]
    mean = _segsum(y, ones) * (1.0 / NB)
    d = y - mean
    var = _segsum(d * d, ones) * (1.0 / NB)
    yb = (d * lax.rsqrt(var + GN_EPS) * gng_ref[...] + gnb_ref[...] + bv_ref[...]) * g_ref[...]
    aw = ya_ref.shape[1]
    o_ref[...] = x_ref[...] + _dot(ya_ref[...].astype(BF16), w_ref[:aw, :]) + _dot(yb.astype(BF16), w_ref[aw:, :])


def mix0_out(ya, y, bv, g, x, gn_g, gn_b, ones_bf, w_bf):
    T, D = x.shape
    aw = ya.shape[1]
    bw = y.shape[1]
    tm = min(256, T)
    tok = lambda n: pl.BlockSpec((tm, n), lambda i: (i, 0))
    return pl.pallas_call(
        _mix0_out_body,
        grid=(T // tm,),
        in_specs=[tok(aw), tok(bw), tok(bw), tok(bw), tok(D), _full((1, bw)), _full((1, bw)), _full(ones_bf.shape),
                  _full(w_bf.shape)],
        out_specs=tok(D),
        out_shape=jax.ShapeDtypeStruct((T, D), F32),
        compiler_params=_params("arbitrary"), name="mix0_out")(ya, y, bv, g, x, gn_g.reshape(1, bw), gn_b.reshape(1, bw),
                                                              ones_bf, w_bf)


def _mix1_out_body(a1_ref, a2_ref, x_ref, w_ref, o_ref):
    n1 = a1_ref.shape[1]
    o_ref[...] = x_ref[...] + _dot(a1_ref[...].astype(BF16), w_ref[:n1, :]) + _dot(a2_ref[...].astype(BF16), w_ref[n1:, :])


def mix1_out(a1, a2, x, w_bf):
    T, D = x.shape
    tm = min(256, T)
    tok = lambda n: pl.BlockSpec((tm, n), lambda i: (i, 0))
    return pl.pallas_call(
        _mix1_out_body,
        grid=(T // tm,),
        in_specs=[tok(a1.shape[1]), tok(a2.shape[1]), tok(D), _full(w_bf.shape)],
        out_specs=tok(D),
        out_shape=jax.ShapeDtypeStruct((T, D), F32),
        compiler_params=_params("arbitrary"), name="mix1_out")(a1, a2, x, w_bf)


def _xattn_body(x_ref, g_ref, wq_ref, mk_ref, mv_ref, wo_ref, o_ref, *, nh):
    x = x_ref[0]
    h = _rms(x, g_ref[...]).astype(BF16)
    q = _dot(h, wq_ref[...])
    mk = mk_ref[0].astype(BF16)
    mv = mv_ref[0].astype(BF16)
    hd = q.shape[1] // nh
    outs = []
    for hh in range(nh):
        sl = slice(hh * hd, (hh + 1) * hd)
        lg = _dot_nt(q[:, sl].astype(BF16), mk[:, sl]) * (hd ** -0.5)
        e = jnp.exp(lg - jnp.max(lg, -1, keepdims=True))
        p = e / jnp.sum(e, -1, keepdims=True)
        outs.append(_dot(p.astype(BF16), mv[:, sl]))
    o = jnp.concatenate(outs, axis=1).astype(BF16)
    o_ref[0] = x + _dot(o, wo_ref[...])


def cross_attn(x3, g, wq_bf, mk, mv, wo_bf):
    B, S, D = x3.shape
    M = mk.shape[1]
    tm = min(256, S)
    return pl.pallas_call(
        functools.partial(_xattn_body, nh=XH),
        grid=(B, S // tm),
        in_specs=[pl.BlockSpec((1, tm, D), lambda b, j: (b, j, 0)), _full((1, D)), _full(wq_bf.shape),
                  pl.BlockSpec((1, M, D), lambda b, j: (b, 0, 0)), pl.BlockSpec((1, M, D), lambda b, j: (b, 0, 0)),
                  _full(wo_bf.shape)],
        out_specs=pl.BlockSpec((1, tm, D), lambda b, j: (b, j, 0)),
        out_shape=jax.ShapeDtypeStruct((B, S, D), F32),
        compiler_params=_params("arbitrary", "arbitrary"), name="cross_attn")(x3, g.reshape(1, D), wq_bf, mk, mv, wo_bf)


def _ffn_body(x_ref, g_ref, wg_ref, wu_ref, wd_ref, gf_ref, o_ref, *, final_norm):
    x = x_ref[...]
    h = _rms(x, g_ref[...]).astype(BF16)
    gate = _dot(h, wg_ref[...])
    up = _dot(h, wu_ref[...])
    act = (jax.nn.silu(gate) * up).astype(BF16)
    y = x + _dot(act, wd_ref[...])
    if final_norm:
        y = _rms(y, gf_ref[...])
    o_ref[...] = y


def ffn(x, g, wg_bf, wu_bf, wd_bf, g_final, final_norm):
    T, D = x.shape
    tm = min(256, T)
    return pl.pallas_call(
        functools.partial(_ffn_body, final_norm=final_norm),
        grid=(T // tm,),
        in_specs=[pl.BlockSpec((tm, D), lambda i: (i, 0)), _full((1, D)), _full(wg_bf.shape), _full(wu_bf.shape),
                  _full(wd_bf.shape), _full((1, D))],
        out_specs=pl.BlockSpec((tm, D), lambda i: (i, 0)),
        out_shape=jax.ShapeDtypeStruct((T, D), F32),
        compiler_params=_params("arbitrary"), name="ffn")(x, g.reshape(1, D), wg_bf, wu_bf, wd_bf, g_final.reshape(1, D))


def _shift_rows(x, d, fill):
    row = lax.broadcasted_iota(I32, x.shape, 0)
    return jnp.where(row >= d, pltpu.roll(x, d, axis=0), fill)


def _lru_body(pc_ref, h0_ref, cv0_ref, cw_ref, cb_ref, wx_ref, bx_ref, wa_ref, ba_ref, lam_ref,
              y_out, h_out, cv_out, hc_ref, cc_ref, *, reset_first):
    j = pl.program_id(1)
    pc = pc_ref[0]
    tm = pc.shape[0]
    cw = pc.shape[1] // 2
    gate_pre = pc[:, :cw]
    x = pc[:, cw:]

    @pl.when(j == 0)
    def _():
        hc_ref[...] = h0_ref[0]
        cc_ref[...] = cv0_ref[0]

    carry = cc_ref[...]
    row8 = lax.broadcasted_iota(I32, (SUBLANES, cw), 0)
    y = cb_ref[...] + x * cw_ref[CONV_W - 1:CONV_W, :]
    for d in range(1, CONV_W):
        xs = pltpu.roll(x, d, axis=0)
        first = jnp.where(row8 < d, pltpu.roll(carry, d, axis=0), xs[:SUBLANES])
        xd = first if tm == SUBLANES else jnp.concatenate([first, xs[SUBLANES:]], axis=0)
        y = y + xd * cw_ref[CONV_W - 1 - d:CONV_W - d, :]
    new_carry = x[tm - SUBLANES:, :]
    cc_ref[...] = new_carry
    cv_out[0] = new_carry

    yb = y.astype(BF16)
    gx = jax.nn.sigmoid(_dot(yb, wx_ref[...]) + bx_ref[...])
    ga = jax.nn.sigmoid(_dot(yb, wa_ref[...]) + ba_ref[...])
    log_a = LRU_C * ga * (-_softplus(-lam_ref[...]))
    a = jnp.exp(log_a)
    th = jnp.tanh(log_a)
    mult = jnp.sqrt(-2.0 * th / (1.0 - th))
    if reset_first:
        row = lax.broadcasted_iota(I32, (tm, cw), 0)
        mult = jnp.where((row == 0) & (j == 0), 1.0, mult)
    bt = y * gx * mult
    d = 1
    while d < tm:
        bt = a * _shift_rows(bt, d, 0.0) + bt
        a = a * _shift_rows(a, d, 1.0)
        d *= 2
    h = bt + a * hc_ref[...]
    h_last = h[tm - 1:tm, :]
    hc_ref[...] = h_last
    h_out[0] = h_last
    y_out[0] = h * jax.nn.gelu(gate_pre)


def lru_mixer(pc3, h0, conv0, conv_w, conv_b, wx_bd, bx, wa_bd, ba, lam, reset_first):
    B, S, W = pc3.shape
    cw = W // 2
    tm = min(256, S)
    cv0 = jnp.concatenate([jnp.zeros((B, SUBLANES - (CONV_W - 1), cw), F32), conv0], axis=1)
    row = lambda a: a.reshape(1, cw)
    y, h_last, cv = pl.pallas_call(
        functools.partial(_lru_body, reset_first=reset_first),
        grid=(B, S // tm),
        in_specs=[pl.BlockSpec((1, tm, W), lambda b, j: (b, j, 0)), pl.BlockSpec((1, 1, cw), lambda b, j: (b, 0, 0)),
                  pl.BlockSpec((1, SUBLANES, cw), lambda b, j: (b, 0, 0)), _full((CONV_W, cw)), _full((1, cw)),
                  _full(wx_bd.shape), _full((1, cw)), _full(wa_bd.shape), _full((1, cw)), _full((1, cw))],
        out_specs=[pl.BlockSpec((1, tm, cw), lambda b, j: (b, j, 0)), pl.BlockSpec((1, 1, cw), lambda b, j: (b, 0, 0)),
                   pl.BlockSpec((1, SUBLANES, cw), lambda b, j: (b, 0, 0))],
        out_shape=[jax.ShapeDtypeStruct((B, S, cw), F32), jax.ShapeDtypeStruct((B, 1, cw), F32),
                   jax.ShapeDtypeStruct((B, SUBLANES, cw), F32)],
        scratch_shapes=[pltpu.VMEM((1, cw), F32), pltpu.VMEM((SUBLANES, cw), F32)],
        compiler_params=_params("arbitrary", "arbitrary"), name="lru_mixer")(
            pc3, h0.reshape(B, 1, cw), cv0, conv_w, row(conv_b), wx_bd, row(bx), wa_bd, row(ba), row(lam))
    return y, h_last[:, 0], cv[:, SUBLANES - (CONV_W - 1):]


def _rope_tables(pos, head_dim):
    rot = head_dim // 4
    half = rot // 2
    inv = ROPE_THETA ** (-jnp.arange(half, dtype=F32) * 2.0 / rot)
    ang = pos.astype(F32)[:, None] * inv[None, :]
    cos, sin = jnp.cos(ang), jnp.sin(ang)
    S = pos.shape[0]
    rest = head_dim - rot
    c = jnp.concatenate([cos, cos, jnp.ones((S, rest), F32)], -1)
    sa = jnp.concatenate([-sin, jnp.zeros((S, head_dim - half), F32)], -1)
    sb = jnp.concatenate([jnp.zeros((S, half), F32), sin, jnp.zeros((S, rest), F32)], -1)
    rep = LANES // head_dim
    return tuple(jnp.tile(t, (1, rep)) for t in (c, sa, sb))


def _rope(x, c, sa, sb, half):
    n = x.shape[1] // LANES
    tile = (lambda t: t) if n == 1 else (lambda t: jnp.concatenate([t] * n, axis=1))
    w = x.shape[1]
    return x * tile(c) + pltpu.roll(x, w - half, axis=1) * tile(sa) + pltpu.roll(x, half, axis=1) * tile(sb)


def _rope_all(q_ref, k_ref, qi_ref, kiw_ref, cq_ref, saq_ref, sbq_ref, ci_ref, sai_ref, sbi_ref):
    qr = _rope(q_ref[...], cq_ref[...], saq_ref[...], sbq_ref[...], HD // 8)
    kr = _rope(k_ref[...], cq_ref[...], saq_ref[...], sbq_ref[...], HD // 8)
    qir = _rope(qi_ref[...], ci_ref[...], sai_ref[...], sbi_ref[...], D_I // 8)
    kiw = kiw_ref[...]
    lane = lax.broadcasted_iota(I32, kiw.shape, 1)
    iski = lane < D_I
    kiwr = _rope(kiw, jnp.where(iski, ci_ref[...], 1.0), jnp.where(iski, sai_ref[...], 0.0),
                 jnp.where(iski, sbi_ref[...], 0.0), D_I // 8)
    return qr, kr, qir, kiwr


def _dsa_prep_sample_body(q_ref, k_ref, qi_ref, kiw_ref, cq_ref, saq_ref, sbq_ref, ci_ref, sai_ref, sbi_ref,
                          qr_out, kr_out, qir_out, kiwr_out):
    qr, kr, qir, kiwr = _rope_all(q_ref, k_ref, qi_ref, kiw_ref, cq_ref, saq_ref, sbq_ref, ci_ref, sai_ref, sbi_ref)
    qr_out[...] = qr
    kr_out[...] = kr
    qir_out[...] = qir
    kiwr_out[...] = kiwr


def _dsa_prep_prompt_body(q_ref, k_ref, v_ref, qi_ref, kiw_ref, cq_ref, saq_ref, sbq_ref, ci_ref, sai_ref, sbi_ref,
                          kr_out, kiwr_out, wq_out, wi_out, wrow_out, kbf_out, kib_out, vt_out):
    qr, kr, qir, kiwr = _rope_all(q_ref, k_ref, qi_ref, kiw_ref, cq_ref, saq_ref, sbq_ref, ci_ref, sai_ref, sbi_ref)
    tm = qr.shape[0]
    kr_out[...] = kr
    kiwr_out[...] = kiwr
    kbf_out[...] = kr.astype(BF16)
    kib_out[...] = kiwr.astype(BF16)
    vt_out[0] = v_ref[...].T.astype(BF16)
    qt = (qr * (HD ** -0.5 * LOG2E)).T
    z = jnp.zeros((HD, tm), F32)
    slabs = []
    for h in range(H_D):
        piece = qt[h * HD:(h + 1) * HD, :]
        slabs.append(jnp.concatenate([piece, z] if h < H_D // N_KV else [z, piece], axis=0))
    wq_out[0] = jnp.concatenate(slabs, axis=1).astype(BF16)
    qit = qir.T
    zi = jnp.zeros((LANES - D_I, tm), F32)
    wi_out[0] = jnp.concatenate([jnp.concatenate([qit[h * D_I:(h + 1) * D_I, :], zi], axis=0) for h in range(H_I)],
                                axis=1).astype(BF16)
    wrow_out[0] = kiwr.T[D_I:D_I + H_I, :] * (H_I ** -0.5 * D_I ** -0.5)


def dsa_prep(q, k, v, qi, kiw, tabs_q, tabs_i, prompt):
    T = q.shape[0]
    tm = Q_BLOCK if prompt else min(256, T)
    tok = lambda n: pl.BlockSpec((tm, n), lambda i: (i, 0))
    nq, nk, ni = q.shape[1], k.shape[1], qi.shape[1]
    tabs = list(tabs_q) + list(tabs_i)
    if not prompt:
        return pl.pallas_call(
            _dsa_prep_sample_body,
            grid=(T // tm,),
            in_specs=[tok(nq), tok(nk), tok(ni), tok(LANES)] + [tok(LANES)] * 6,
            out_specs=[tok(nq), tok(nk), tok(ni), tok(LANES)],
            out_shape=[jax.ShapeDtypeStruct((T, n), F32) for n in (nq, nk, ni, LANES)],
            compiler_params=_params("arbitrary"), name="dsa_prep_sample")(q, k, qi, kiw, *tabs)
    nblk = T // tm
    blk = lambda r, c: pl.BlockSpec((1, r, c), lambda i: (i, 0, 0))
    nper = tabs[0].shape[0] // tm
    return pl.pallas_call(
        _dsa_prep_prompt_body,
        grid=(nblk,),
        in_specs=[tok(nq), tok(nk), tok(nk), tok(ni), tok(LANES)] + [pl.BlockSpec((tm, LANES), lambda i: (i % nper, 0))] * 6,
        out_specs=[tok(nk), tok(LANES), blk(nk, H_D * tm), blk(LANES, H_I * tm), blk(H_I, tm), tok(nk), tok(LANES),
                   blk(nk, tm)],
        out_shape=[jax.ShapeDtypeStruct((T, nk), F32), jax.ShapeDtypeStruct((T, LANES), F32),
                   jax.ShapeDtypeStruct((nblk, nk, H_D * tm), BF16), jax.ShapeDtypeStruct((nblk, LANES, H_I * tm), BF16),
                   jax.ShapeDtypeStruct((nblk, H_I, tm), F32), jax.ShapeDtypeStruct((T, nk), BF16),
                   jax.ShapeDtypeStruct((T, LANES), BF16), jax.ShapeDtypeStruct((nblk, nk, tm), BF16)],
        compiler_params=_params("arbitrary"), name="dsa_prep_prompt")(q, k, v, qi, kiw, *tabs)


def _key_to_f32(k):
    return pltpu.bitcast(k ^ ((k >> 31) & 0x7FFFFFFF), F32)


def _kth_largest(count_ge, shape, k):
    def step(it, key):
        cand = key + jnp.left_shift(jnp.int32(1), 31 - it)
        return jnp.where(count_ge(_key_to_f32(cand)) >= k, cand, key)

    key = lax.fori_loop(0, 32, step, jnp.full(shape, INT_MIN, I32))
    return jnp.where(key <= KEY_NEG_INF, -jnp.inf, _key_to_f32(key))


def _dsa_prompt_body(wq_ref, wi_ref, wrow_ref, kbf_ref, kib_ref, vt_ref, o_ref,
                     sc_ref, acc_ref, bias_ref, *, topk, kbs):
    i = pl.program_id(1)
    qb = Q_BLOCK
    q0 = i * qb
    ls = kbs * qb
    la = ls // 2
    ns = (q0 + qb + ls - 1) // ls
    sub = lax.broadcasted_iota(I32, (qb, qb), 0)
    lane = lax.broadcasted_iota(I32, (qb, qb), 1)
    wi = wi_ref[0]
    wrow = wrow_ref[0]

    def score_block(kb, carry):
        k0 = kb * ls
        r = _dot(kib_ref[0, pl.ds(k0, ls), :], wi)
        for c in range(kbs):
            rc = r[c * qb:(c + 1) * qb, :]
            acc = wrow[0:1, :] * jnp.maximum(rc[:, 0:qb], 0.0)
            for h in range(1, H_I):
                acc = acc + wrow[h:h + 1, :] * jnp.maximum(rc[:, h * qb:(h + 1) * qb], 0.0)
            causal = (k0 + c * qb + sub) <= (q0 + lane)
            sc_ref[pl.ds(k0 + c * qb, qb), :] = jnp.where(causal, acc, -jnp.inf)
        return carry

    lax.fori_loop(0, ns, score_block, 0)

    def count(pred):
        def body(kb, cnt):
            for c in range(kbs):
                m = jnp.where(pred(sc_ref[pl.ds(kb * ls + c * qb, qb), :]), 1, 0)
                cnt = cnt + jnp.sum(m.reshape(qb // SUBLANES, SUBLANES, qb), axis=0)
            return cnt
        cnt = lax.fori_loop(0, ns, body, jnp.zeros((SUBLANES, qb), I32))
        return jnp.sum(cnt, axis=0, keepdims=True)

    thr = _kth_largest(lambda t: count(lambda s: s >= t), (1, qb), topk)
    need = (topk - count(lambda s: s > thr)).astype(F32)
    excess = (count(lambda s: s >= thr) > topk) & (thr > -jnp.inf)

    @pl.when(jnp.max(jnp.where(excess, 1, 0)) > 0)
    def _():
        ltri = jnp.where(lane < sub, 1.0, 0.0).astype(BF16)

        def body(c, run):
            s = sc_ref[pl.ds(c * qb, qb), :]
            eq = s == thr
            eqf = jnp.where(eq, 1.0, 0.0)
            pre = _dot(ltri, eqf.astype(BF16)) + run
            sc_ref[pl.ds(c * qb, qb), :] = jnp.where(eq & (pre >= need), -jnp.inf, s)
            return run + jnp.sum(eqf, axis=0, keepdims=True)

        lax.fori_loop(0, ns * kbs, body, jnp.zeros((1, qb), F32))

    acc_ref[...] = jnp.zeros(acc_ref.shape, F32)
    hpg = H_D // N_KV

    def attend(kb, carry):
        ms, ls_ = carry
        logits = _dot(kbf_ref[0, pl.ds(kb * la, la), :], wq_ref[0])
        s = sc_ref[pl.ds(kb * la, la), :]
        bias_ref[...] = jnp.where((s >= thr) & (s > -jnp.inf), 0.0, NEG)
        vt = vt_ref[0, kb]
        ms_new, ls_new = [], []
        for g in range(N_KV):
            ps, alphas = [], []
            for h in range(g * hpg, (g + 1) * hpg):
                lg = logits[:, h * qb:(h + 1) * qb] + bias_ref[...]
                m_new = jnp.maximum(ms[h], jnp.max(lg, axis=0, keepdims=True))
                p = jnp.exp2(lg - m_new)
                alpha = jnp.exp2(ms[h] - m_new)
                ms_new.append(m_new)
                ls_new.append(alpha * ls_[h] + jnp.sum(p, axis=0, keepdims=True))
                ps.append(p.astype(BF16))
                alphas.append(alpha)
            gs = slice(g * hpg * qb, (g + 1) * hpg * qb)
            acc_ref[:, gs] = (acc_ref[:, gs] * jnp.concatenate(alphas, axis=1)
                              + _dot(vt[g * HD:(g + 1) * HD, :], jnp.concatenate(ps, axis=1)))
        return tuple(ms_new), tuple(ls_new)

    init = (tuple(jnp.full((1, qb), NEG, F32) for _ in range(H_D)), tuple(jnp.zeros((1, qb), F32) for _ in range(H_D)))
    _, l_fin = lax.fori_loop(0, (q0 + qb + la - 1) // la, attend, init)
    o_ref[0] = jnp.concatenate([acc_ref[:, h * qb:(h + 1) * qb] * (1.0 / l_fin[h]) for h in range(H_D)], axis=0).T


def dsa_prompt(wq, wi, wrow, kbf, kib, vt, B, S):
    nq = S // Q_BLOCK
    topk = min(TOPK_MAX, S // 4)
    assert nq % 2 == 0
    kbs = 4 if nq % 4 == 0 else 2
    kba = kbs // 2
    la = kba * Q_BLOCK
    nk = kbf.shape[-1]
    vt4 = vt.reshape(B, S // la, kba, nk, Q_BLOCK).transpose(0, 1, 3, 2, 4).reshape(B, S // la, nk, la)
    qspec = lambda r, c: pl.BlockSpec((1, r, c), lambda b, i: (b * nq + i, 0, 0))
    return pl.pallas_call(
        functools.partial(_dsa_prompt_body, topk=topk, kbs=kbs),
        grid=(B, nq),
        in_specs=[qspec(nk, H_D * Q_BLOCK), qspec(LANES, H_I * Q_BLOCK), qspec(H_I, Q_BLOCK),
                  pl.BlockSpec((1, S, nk), lambda b, i: (b, 0, 0)), pl.BlockSpec((1, S, LANES), lambda b, i: (b, 0, 0)),
                  pl.BlockSpec((1, S // la, nk, la), lambda b, i: (b, 0, 0, 0))],
        out_specs=pl.BlockSpec((1, Q_BLOCK, H_D * HD), lambda b, i: (b, i, 0)),
        out_shape=jax.ShapeDtypeStruct((B, S, H_D * HD), F32),
        scratch_shapes=[pltpu.VMEM((S, Q_BLOCK), F32), pltpu.VMEM((HD, H_D * Q_BLOCK), F32),
                        pltpu.VMEM((la, Q_BLOCK), F32)],
        compiler_params=_params("arbitrary", "arbitrary"), name="dsa_prompt")(
            wq, wi, wrow, kbf.reshape(B, S, nk), kib.reshape(B, S, LANES), vt4)


def _page_copies(pt_ref, pool_ref, buf_ref, sem_ref, step, slot, pps):
    return [pltpu.make_async_copy(pool_ref.at[pt_ref[step * pps + i]], buf_ref.at[slot, i], sem_ref.at[slot])
            for i in range(pps)]


def _index_scores(qi, wb, pages_bf, t):
    r = _dot_nt(qi, pages_bf)
    acc = wb[0:t, :] * jnp.maximum(r[0:t, :], 0.0)
    for h in range(1, H_I):
        acc = acc + wb[h * t:(h + 1) * t, :] * jnp.maximum(r[h * t:(h + 1) * t, :], 0.0)
    return acc


def _dsa_s_scores_body(pt_ref, qi_ref, wb_ref, new_ref, pool_ref, s_ref, snew_ref, buf_ref, sem_ref, *, pps, nc, t):
    s = pl.program_id(0)
    slot = s % 2
    copies = functools.partial(_page_copies, pt_ref, pool_ref, buf_ref, sem_ref, pps=pps)

    @pl.when(s == 0)
    def _():
        for cp in copies(s, slot):
            cp.start()

    @pl.when(s + 1 < pl.num_programs(0))
    def _():
        for cp in copies(s + 1, 1 - slot):
            cp.start()

    qi = qi_ref[0]
    wb = wb_ref[0]

    @pl.when(s % nc == 0)
    def _():
        acc = _index_scores(qi, wb, new_ref[0].astype(BF16), t)
        sub = lax.broadcasted_iota(I32, acc.shape, 0)
        lane = lax.broadcasted_iota(I32, acc.shape, 1)
        snew_ref[0] = jnp.where(lane <= sub, acc, -jnp.inf)

    for cp in copies(s, slot):
        cp.wait()
    pages = buf_ref[slot]
    for i in range(pps):
        s_ref[0, i] = _index_scores(qi, wb, pages[i].astype(BF16), t)


def _dsa_s_attend_body(pt_ref, qm_ref, sp_ref, sn_ref, knew_ref, vnew_ref, kpool_ref, vpool_ref, o_ref,
                       kbuf_ref, vbuf_ref, sem_ref, sc_ref, thr_ref, m_ref, l_ref, acc_ref, *, pps, nc, t, topk):
    s = pl.program_id(0)
    c = s % nc
    slot = s % 2
    npg = nc * pps
    kcopies = functools.partial(_page_copies, pt_ref, kpool_ref, kbuf_ref, sem_ref.at[0], pps=pps)
    vcopies = functools.partial(_page_copies, pt_ref, vpool_ref, vbuf_ref, sem_ref.at[1], pps=pps)

    @pl.when(s == 0)
    def _():
        for cp in kcopies(s, slot) + vcopies(s, slot):
            cp.start()

    @pl.when(s + 1 < pl.num_programs(0))
    def _():
        for cp in kcopies(s + 1, 1 - slot) + vcopies(s + 1, 1 - slot):
            cp.start()

    @pl.when(c == 0)
    def _():
        sc_ref[0:npg] = sp_ref[0]
        sc_ref[npg] = sn_ref[0]

        def count(pred):
            cnt = jnp.zeros((t, PAGE_SIZE), I32)
            for ch in range(npg + 1):
                cnt = cnt + jnp.where(pred(sc_ref[ch]), 1, 0)
            return jnp.sum(cnt, axis=1, keepdims=True)

        thr = _kth_largest(lambda x: count(lambda sc: sc >= x), (t, 1), topk)
        need = (topk - count(lambda sc: sc > thr)).astype(F32)
        excess = (count(lambda sc: sc >= thr) > topk) & (thr > -jnp.inf)

        @pl.when(jnp.max(jnp.where(excess, 1, 0)) > 0)
        def _():
            sub = lax.broadcasted_iota(I32, (PAGE_SIZE, PAGE_SIZE), 0)
            lane = lax.broadcasted_iota(I32, (PAGE_SIZE, PAGE_SIZE), 1)
            utri = jnp.where(sub < lane, 1.0, 0.0).astype(BF16)

            def body(ch, run):
                sc = sc_ref[ch]
                eq = sc == thr
                eqf = jnp.where(eq, 1.0, 0.0)
                pre = _dot(eqf.astype(BF16), utri) + run
                sc_ref[ch] = jnp.where(eq & (pre >= need), -jnp.inf, sc)
                return run + jnp.sum(eqf, axis=1, keepdims=True)

            lax.fori_loop(0, npg + 1, body, jnp.zeros((t, 1), F32))

        thr_ref[...] = jnp.broadcast_to(thr, thr_ref.shape)
        m_ref[...] = jnp.full(m_ref.shape, NEG, F32)
        l_ref[...] = jnp.zeros(l_ref.shape, F32)
        acc_ref[...] = jnp.zeros(acc_ref.shape, F32)

    qm = qm_ref[0]
    thr = thr_ref[...]

    def bias_rows(sc):
        b = jnp.where((sc >= thr) & (sc > -jnp.inf), 0.0, NEG)
        return jnp.concatenate([b] * H_D, axis=0)

    def update(k_bf, v_bf, bias):
        lg = _dot_nt(qm, k_bf) + bias
        m_old = m_ref[...]
        m_new = jnp.maximum(m_old, jnp.max(lg, axis=1, keepdims=True))
        p = jnp.exp2(lg - m_new)
        alpha = jnp.exp2(m_old - m_new)
        m_ref[...] = m_new
        l_ref[...] = alpha * l_ref[...] + jnp.sum(p, axis=1, keepdims=True)
        acc_ref[...] = alpha * acc_ref[...] + _dot(p.astype(BF16), v_bf)

    for cp in kcopies(s, slot) + vcopies(s, slot):
        cp.wait()
    nkv = kbuf_ref.shape[-1]
    bias = jnp.concatenate([bias_rows(sc_ref[c * pps + i]) for i in range(pps)], axis=1)
    update(kbuf_ref[slot].reshape(pps * PAGE_SIZE, nkv).astype(BF16),
           vbuf_ref[slot].reshape(pps * PAGE_SIZE, nkv).astype(BF16), bias)

    @pl.when(c == nc - 1)
    def _():
        update(knew_ref[0].astype(BF16), vnew_ref[0].astype(BF16), bias_rows(sc_ref[npg]))
        o_ref[0] = acc_ref[...] / l_ref[...]


def dsa_sample(qr, kr, v, qir, kiwr, pool_k, pool_v, pool_ki, page_table, B, T):
    npg = page_table.shape[1]
    npool = pool_k.shape[0]
    nkv = N_KV * HD
    topk = min(TOPK_MAX, (npg * PAGE_SIZE + T) // 4)
    pps = 32 if npg % 32 == 0 else npg
    nc = npg // pps
    pad_new = lambda a: jnp.pad(a.reshape(B, T, -1), ((0, 0), (0, PAGE_SIZE - T), (0, 0)))
    ki_new = pad_new(kiwr[:, :D_I])
    k_new = pad_new(kr)
    v_new = pad_new(v)
    qi_m = qir.reshape(B, T, H_I, D_I).transpose(0, 2, 1, 3).reshape(B, H_I * T, D_I).astype(BF16)
    wi = kiwr[:, D_I:D_I + H_I].reshape(B, T, H_I) * (H_I ** -0.5 * D_I ** -0.5)
    wb = jnp.broadcast_to(wi.transpose(0, 2, 1).reshape(B, H_I * T, 1), (B, H_I * T, PAGE_SIZE))
    pt_flat = page_table.reshape(-1)
    per_b = lambda r, c: pl.BlockSpec((1, r, c), lambda s, pt: (s // nc, 0, 0))
    hbm = pl.BlockSpec(memory_space=pl.ANY)
    sc_past, sc_new = pl.pallas_call(
        functools.partial(_dsa_s_scores_body, pps=pps, nc=nc, t=T),
        grid_spec=pltpu.PrefetchScalarGridSpec(
            num_scalar_prefetch=1, grid=(B * nc,),
            in_specs=[per_b(H_I * T, D_I), per_b(H_I * T, PAGE_SIZE), per_b(PAGE_SIZE, D_I), hbm],
            out_specs=[pl.BlockSpec((1, pps, T, PAGE_SIZE), lambda s, pt: (s // nc, s % nc, 0, 0)), per_b(T, PAGE_SIZE)],
            scratch_shapes=[pltpu.VMEM((2, pps, PAGE_SIZE, D_I), F32), pltpu.SemaphoreType.DMA((2,))]),
        out_shape=[jax.ShapeDtypeStruct((B, npg, T, PAGE_SIZE), F32), jax.ShapeDtypeStruct((B, T, PAGE_SIZE), F32)],
        compiler_params=_params("arbitrary"), name="dsa_sample_scores")(pt_flat, qi_m, wb, ki_new, pool_ki)
    q4 = (qr * (HD ** -0.5 * LOG2E)).reshape(B, T, N_KV, H_D // N_KV, HD).transpose(0, 2, 3, 1, 4)
    eye = jnp.eye(N_KV, dtype=F32)
    qm = (q4[:, :, :, :, None, :] * eye[None, :, None, None, :, None]).reshape(B, H_D * T, nkv).astype(BF16)
    out = pl.pallas_call(
        functools.partial(_dsa_s_attend_body, pps=pps, nc=nc, t=T, topk=topk),
        grid_spec=pltpu.PrefetchScalarGridSpec(
            num_scalar_prefetch=1, grid=(B * nc,),
            in_specs=[per_b(H_D * T, nkv), pl.BlockSpec((1, npg, T, PAGE_SIZE), lambda s, pt: (s // nc, 0, 0, 0)),
                      per_b(T, PAGE_SIZE), per_b(PAGE_SIZE, nkv), per_b(PAGE_SIZE, nkv), hbm, hbm],
            out_specs=per_b(H_D * T, nkv),
            scratch_shapes=[pltpu.VMEM((2, pps, PAGE_SIZE, nkv), F32), pltpu.VMEM((2, pps, PAGE_SIZE, nkv), F32),
                            pltpu.SemaphoreType.DMA((2, 2)), pltpu.VMEM((npg + 1, T, PAGE_SIZE), F32),
                            pltpu.VMEM((T, PAGE_SIZE), F32), pltpu.VMEM((H_D * T, 1), F32),
                            pltpu.VMEM((H_D * T, 1), F32), pltpu.VMEM((H_D * T, nkv), F32)]),
        out_shape=jax.ShapeDtypeStruct((B, H_D * T, nkv), F32),
        compiler_params=_params("arbitrary"), name="dsa_sample_attend")(
            pt_flat, qm, sc_past, sc_new, k_new, v_new, pool_k.reshape(npool, PAGE_SIZE, nkv),
            pool_v.reshape(npool, PAGE_SIZE, nkv))
    o5 = out.reshape(B, N_KV, H_D // N_KV, T, N_KV, HD)
    og = jnp.stack([o5[:, g, :, :, g, :] for g in range(N_KV)], axis=1)
    return og.transpose(0, 3, 1, 2, 4).reshape(B * T, H_D * HD)


def _block_diag(w):
    n, a, b = w.shape
    eye = jnp.eye(n, dtype=w.dtype)
    return (w[:, :, None, :] * eye[:, None, :, None]).reshape(n * a, n * b)


def _prep_weights(P):
    bf = lambda a: a.astype(BF16)
    bw = HB * NB
    pb_w = 3 * bw + W_LORA + A_LORA + G_LORA
    perm = np.concatenate([np.arange(0, bw), np.arange(bw + W_LORA, 3 * bw + W_LORA), np.arange(bw, bw + W_LORA),
                           np.arange(3 * bw + W_LORA, pb_w)])
    W = {"perm": perm, "inv_perm": np.argsort(perm)}
    aw2 = P["e_w_in"].shape[-1] - pb_w
    W["e_w_in"] = [bf(jnp.concatenate([w[:, :aw2], w[:, aw2:][:, perm]], axis=1)) for w in P["e_w_in"]]
    W["b_mu"] = [m[perm] for m in P["b_mu"]]
    W["wB_pad"] = [bf(jnp.concatenate([w, jnp.zeros((A_LORA, bw), F32)], 0)) for w in P["b_wB"]]
    W["aB_pad"] = [bf(jnp.concatenate([jnp.zeros((W_LORA, bw), F32), w], 0)) for w in P["b_aB"]]
    W["gB"] = [bf(w) for w in P["b_gB"]]
    W["ones_bd"] = bf(_block_diag(jnp.ones((HB, NB, NB), F32)))
    o_in = []
    for w in P["o_w_in"]:
        n_main = w.shape[1] - (D_I + H_I)
        o_in.append(bf(jnp.concatenate([w, jnp.zeros((w.shape[0], LANES - (D_I + H_I)), F32)], axis=1)))
        assert n_main % LANES == 0
    W["o_w_in"] = o_in
    W["wx_bd"] = [bf(_block_diag(w)) for w in P["c_wx"]]
    W["wa_bd"] = [bf(_block_diag(w)) for w in P["c_wa"]]
    for name in ("e_w_out", "o_w_out", "w_xq", "w_xo", "w_ff_gate", "w_ff_up", "w_ff_down"):
        W[name] = [bf(w) for w in P[name]]
    W["w_xkv"] = [bf(jnp.concatenate([k, v], axis=1)) for k, v in zip(P["w_xk"], P["w_xv"])]
    return W


def _memory_kv(mem, norm_mem, W):
    B, M, D = mem.shape
    ks, vs = [], []
    for layer in range(len(W["w_xkv"])):
        k, v = rms_matmul(mem.reshape(B * M, D), norm_mem[layer], W["w_xkv"][layer], (D, D))
        ks.append(k.reshape(B, M, XH, D // XH))
        vs.append(v.reshape(B, M, XH, D // XH))
    return jnp.stack(ks, 0), jnp.stack(vs, 0)


def _trunk(x3, pos0, rwkv_s0, rwkv_shift0, lru_h0, lru_conv0, mem_k, mem_v, P, W, sample_ctx):
    B, S, D = x3.shape
    T = B * S
    prompt = sample_ctx is None
    depth = P["norm_mix"].shape[0]
    x = x3.reshape(T, D)
    pos = pos0 + jnp.arange(S, dtype=I32)
    out = {k: [] for k in ("chunk_v", "rwkv_s", "rwkv_sh", "lru_h", "lru_cv", "att_k", "att_v", "att_ki")}
    bw = HB * NB
    for layer in range(depth):
        j = layer // 2
        if layer % 2 == 0:
            aw2 = W["e_w_in"][j].shape[1] - (3 * bw + W_LORA + A_LORA + G_LORA)
            pa, pb = rms_matmul(x, P["norm_mix"][layer], W["e_w_in"][j], (aw2, W["e_w_in"][j].shape[1] - aw2))
            tril = jnp.tril(P["a_ws"][j])
            if S % CHUNK == 0:
                mix, bias = tril, P["a_bs"][j]
            else:
                assert S <= CHUNK
                eye = jnp.eye(B, dtype=F32)
                mix = (tril[:, None, :S, None, :S] * eye[None, :, None, :, None]).reshape(A_G, T, T)
                bias = jnp.tile(P["a_bs"][j][:, :S], (1, B))
            ac = aw2 // 2 // A_G
            bias_full = jnp.broadcast_to(bias[:, :, None], bias.shape + (ac,))
            mixed = chunk_mixer(pa, P["a_ln_g"][j], P["a_ln_b"][j], mix.astype(BF16), bias_full, emit_v=not prompt)
            ya = mixed[0]
            pbw = pb.shape[1]
            sh0 = rwkv_shift0[j][:, W["perm"]]
            r, dec, k, a, v, g, bv, last = rwkv_prep(
                pb.reshape(B, S, pbw), sh0, W["b_mu"][j], P["b_w0"][j], W["wB_pad"][j], P["b_a0"][j], W["aB_pad"][j],
                W["gB"][j], P["b_ka"][j], P["b_rk"][j].reshape(-1), W["ones_bd"])
            Bg = LANES // (2 * HB)
            G = B // Bg
            grp = lambda t: t.reshape(G, Bg, S, bw)
            y_s, s_last = rwkv_scan(_to_scan(grp(r)), _to_scan(grp(dec)), _to_scan(grp(k)), _to_scan(grp(a)),
                                    _to_scan(grp(v)), _state_to_scan(rwkv_s0[j].astype(F32).reshape(G, Bg, HB, NB, NB)),
                                    _param_to_scan(P["b_kk"][j], Bg), _param_to_scan(P["b_ka"][j], Bg))
            y = _from_scan(y_s, Bg).reshape(T, bw)
            x = mix0_out(ya, y, bv.reshape(T, bw), g.reshape(T, bw), x, P["b_gn_g"][j], P["b_gn_b"][j], W["ones_bd"],
                         W["e_w_out"][j])
            out["chunk_v"].append(mixed[-1].reshape(B, S, -1))
            out["rwkv_s"].append(_state_from_scan(s_last, Bg).reshape(B, HB, NB, NB))
            out["rwkv_sh"].append(last[:, 0][:, W["inv_perm"]])
        else:
            cw2 = 2 * P["c_conv_b"].shape[-1]
            nq, nk, ni = H_D * HD, N_KV * HD, H_I * D_I
            pc, q, k, v, qi, kiw = rms_matmul(x, P["norm_mix"][layer], W["o_w_in"][j], (cw2, nq, nk, nk, ni, LANES))
            yc, h_last, conv_new = lru_mixer(pc.reshape(B, S, cw2), lru_h0[j], lru_conv0[j], P["c_conv_w"][j],
                                             P["c_conv_b"][j], W["wx_bd"][j], P["c_bx"][j], W["wa_bd"][j], P["c_ba"][j],
                                             P["c_lambda"][j], reset_first=prompt)
            tabs_q, tabs_i = _rope_tables(pos, HD), _rope_tables(pos, D_I)
            if not prompt:
                tabs_q, tabs_i = (tuple(jnp.tile(t, (B, 1)) for t in tabs) for tabs in (tabs_q, tabs_i))
            if prompt:
                kr, kiwr, wq, wi, wrow, kbf, kib, vt = dsa_prep(q, k, v, qi, kiw, tabs_q, tabs_i, True)
                yd = dsa_prompt(wq, wi, wrow, kbf, kib, vt, B, S).reshape(T, nq)
            else:
                qr, kr, qir, kiwr = dsa_prep(q, k, v, qi, kiw, tabs_q, tabs_i, False)
                pool_k, pool_v, pool_ki, page_table = sample_ctx
                yd = dsa_sample(qr, kr, v, qir, kiwr, pool_k[j], pool_v[j], pool_ki[j], page_table, B, S)
            x = mix1_out(yc.reshape(T, -1), yd, x, W["o_w_out"][j])
            out["lru_h"].append(h_last)
            out["lru_cv"].append(conv_new)
            out["att_k"].append(kr.reshape(B, S, N_KV, HD))
            out["att_v"].append(v.reshape(B, S, N_KV, HD))
            out["att_ki"].append(kiwr[:, :D_I].reshape(B, S, D_I))
        M = mem_k.shape[2]
        x = cross_attn(x.reshape(B, S, D), P["norm_x"][layer], W["w_xq"][layer], mem_k[layer].reshape(B, M, D),
                       mem_v[layer].reshape(B, M, D), W["w_xo"][layer]).reshape(T, D)
        x = ffn(x, P["norm_ffn"][layer], W["w_ff_gate"][layer], W["w_ff_up"][layer], W["w_ff_down"][layer],
                P["norm_final"], final_norm=(layer == depth - 1))
    st = lambda l: jnp.stack(l, 0)
    return (x.reshape(B, S, D),) + tuple(st(out[k]) for k in
                                         ("chunk_v", "rwkv_s", "rwkv_sh", "lru_h", "lru_cv", "att_k", "att_v", "att_ki"))


def kernel(x_prompt, x_sample, mem_prompt, state_rwkv, state_rwkv_shift, state_lru_h, state_lru_conv, cache_attn_k, cache_attn_v, cache_attn_kidx, cache_mem_k, cache_mem_v, page_table, norm_mix, norm_x, norm_mem, norm_ffn, norm_final, w_xq, w_xk, w_xv, w_xo, w_ff_gate, w_ff_up, w_ff_down, e_w_in, e_w_out, a_ln_g, a_ln_b, a_ws, a_bs, b_mu, b_w0, b_wB, b_a0, b_aB, b_gB, b_kk, b_ka, b_rk, b_gn_g, b_gn_b, o_w_in, o_w_out, c_conv_w, c_conv_b, c_wx, c_bx, c_wa, c_ba, c_lambda):
    P = dict(norm_mix=norm_mix, norm_x=norm_x, norm_ffn=norm_ffn, norm_final=norm_final,
             w_xq=w_xq, w_xk=w_xk, w_xv=w_xv, w_xo=w_xo, w_ff_gate=w_ff_gate, w_ff_up=w_ff_up, w_ff_down=w_ff_down,
             e_w_in=e_w_in, e_w_out=e_w_out, a_ln_g=a_ln_g, a_ln_b=a_ln_b, a_ws=a_ws, a_bs=a_bs,
             b_mu=b_mu, b_w0=b_w0, b_wB=b_wB, b_a0=b_a0, b_aB=b_aB, b_gB=b_gB, b_kk=b_kk, b_ka=b_ka,
             b_rk=b_rk, b_gn_g=b_gn_g, b_gn_b=b_gn_b,
             o_w_in=o_w_in, o_w_out=o_w_out, c_conv_w=c_conv_w, c_conv_b=c_conv_b,
             c_wx=c_wx, c_bx=c_bx, c_wa=c_wa, c_ba=c_ba, c_lambda=c_lambda)
    W = _prep_weights(P)
    dt = x_prompt.dtype
    bp = x_prompt.shape[0]
    n_even, n_odd = state_rwkv.shape[0], state_lru_h.shape[0]
    pb_w = state_rwkv_shift.shape[-1]
    cw = state_lru_h.shape[-1]

    p_mem_k, p_mem_v = _memory_kv(mem_prompt, norm_mem, W)
    (y_prompt, _, p_rwkv_state, p_rwkv_shift, p_lru_h, p_lru_conv, p_attn_k, p_attn_v, p_attn_kidx) = _trunk(
        x_prompt, 0,
        jnp.zeros((n_even, bp, HB, NB, NB), dt), jnp.zeros((n_even, bp, pb_w), dt),
        jnp.zeros((n_odd, bp, cw), dt), jnp.zeros((n_odd, bp, CONV_W - 1, cw), dt),
        p_mem_k, p_mem_v, P, W, None)

    past_len = page_table.shape[1] * PAGE_SIZE
    (y_sample, s_chunk_v, s_rwkv_state, s_rwkv_shift, s_lru_h, s_lru_conv, s_attn_k, s_attn_v, s_attn_kidx) = _trunk(
        x_sample, past_len, state_rwkv, state_rwkv_shift, state_lru_h, state_lru_conv,
        cache_mem_k, cache_mem_v, P, W, (cache_attn_k, cache_attn_v, cache_attn_kidx, page_table))

    return (y_prompt, y_sample,
            p_rwkv_state, p_rwkv_shift, p_lru_h, p_lru_conv, p_attn_k, p_attn_v, p_attn_kidx, p_mem_k, p_mem_v,
            s_chunk_v, s_rwkv_state, s_rwkv_shift, s_lru_h, s_lru_conv, s_attn_k, s_attn_v, s_attn_kidx)
```

```python
import functools

import numpy as np
import jax
import jax.numpy as jnp
from jax import lax
from jax.experimental import pallas as pl
from jax.experimental.pallas import tpu as pltpu

F32 = jnp.float32
BF16 = jnp.bfloat16
I32 = jnp.int32
INT_MIN = -2 ** 31

EPS = 1e-6
GN_EPS = 64e-5
CHUNK = 128
A_G = 4
HB = 8
NB = 64
W_LORA = 64
A_LORA = 64
G_LORA = 128
NBLK = 8
CONV_W = 4
LRU_C = 8.0
H_D = 8
HD = 64
N_KV = 2
H_I = 8
D_I = 32
TOPK_MAX = 256
Q_BLOCK = 128
PAGE_SIZE = 128
ROPE_THETA = 500000.0
XH = 4
LANES = 128
SUBLANES = 8
VMEM_LIMIT = 56 * 1024 * 1024
NEG = -1e30
LOG2E = 1.4426950408889634
KEY_NEG_INF = -2139095041


def _params(*sem):
    return pltpu.CompilerParams(dimension_semantics=sem, vmem_limit_bytes=VMEM_LIMIT)


def _dot(a, b):
    return jnp.dot(a, b, preferred_element_type=F32)


def _dot_nt(a, b):
    return lax.dot_general(a, b, (((1,), (1,)), ((), ())), preferred_element_type=F32)


def _rms(x, g):
    return x * lax.rsqrt(jnp.mean(x * x, axis=-1, keepdims=True) + EPS) * g


def _segsum(x, ones_bf):
    hi = x.astype(BF16)
    lo = (x - hi.astype(F32)).astype(BF16)
    return _dot(hi, ones_bf) + _dot(lo, ones_bf)


def _softplus(x):
    return jnp.maximum(x, 0.0) + jnp.log1p(jnp.exp(-jnp.abs(x)))


def _full(shape):
    n = len(shape)
    return pl.BlockSpec(shape, lambda *_: (0,) * n)


def _rms_matmul_body(x_ref, g_ref, w_ref, *out_refs, splits):
    h = _rms(x_ref[...], g_ref[...]).astype(BF16)
    off = 0
    for o_ref, n in zip(out_refs, splits):
        o_ref[...] = _dot(h, w_ref[:, off:off + n])
        off += n


def rms_matmul(x, g, w_bf, splits):
    T, D = x.shape
    N = w_bf.shape[1]
    assert sum(splits) == N
    tm = min(256, T)
    return pl.pallas_call(
        functools.partial(_rms_matmul_body, splits=tuple(splits)),
        grid=(T // tm,),
        in_specs=[pl.BlockSpec((tm, D), lambda i: (i, 0)), _full((1, D)), _full((D, N))],
        out_specs=[pl.BlockSpec((tm, n), lambda i: (i, 0)) for n in splits],
        out_shape=[jax.ShapeDtypeStruct((T, n), F32) for n in splits],
        compiler_params=_params("arbitrary"), name="rms_matmul")(x, g.reshape(1, D), w_bf)


def _chunk_mixer_body(pa_ref, lng_ref, lnb_ref, m_ref, bias_ref, ya_ref, *maybe_v_ref, ng):
    z = jax.nn.gelu(pa_ref[...])
    aw = z.shape[1] // 2
    u = z[:, :aw]
    v = z[:, aw:]
    mu = jnp.mean(v, -1, keepdims=True)
    var = jnp.mean(jnp.square(v - mu), -1, keepdims=True)
    v = (v - mu) * lax.rsqrt(var + EPS) * lng_ref[...] + lnb_ref[...]
    for v_ref in maybe_v_ref:
        v_ref[...] = v
    vb = v.astype(BF16)
    ac = aw // ng
    for g in range(ng):
        s = _dot(m_ref[g], vb[:, g * ac:(g + 1) * ac]) + bias_ref[g]
        ya_ref[:, g * ac:(g + 1) * ac] = u[:, g * ac:(g + 1) * ac] * s


def chunk_mixer(pa, ln_g, ln_b, mix_bf, bias, emit_v):
    T, W = pa.shape
    aw = W // 2
    ng, tm, _ = mix_bf.shape
    nout = 2 if emit_v else 1
    return pl.pallas_call(
        functools.partial(_chunk_mixer_body, ng=ng),
        grid=(T // tm,),
        in_specs=[pl.BlockSpec((tm, W), lambda i: (i, 0)), _full((1, aw)), _full((1, aw)),
                  _full(mix_bf.shape), _full(bias.shape)],
        out_specs=[pl.BlockSpec((tm, aw), lambda i: (i, 0))] * nout,
        out_shape=[jax.ShapeDtypeStruct((T, aw), F32)] * nout,
        compiler_params=_params("arbitrary"), name="chunk_mixer")(pa, ln_g.reshape(1, aw), ln_b.reshape(1, aw), mix_bf, bias)


def _rwkv_prep_body(pb_ref, sh0_ref, mu_ref, w0_ref, wB_ref, a0_ref, aB_ref, gB_ref, ka_ref, rk_ref, ones_ref,
                    r_out, w_out, k_out, a_out, v_out, g_out, bv_out, last_out, carry_ref):
    j = pl.program_id(1)
    pb = pb_ref[0]
    tm = pb.shape[0]
    bw = r_out.shape[2]

    @pl.when(j == 0)
    def _():
        carry_ref[...] = sh0_ref[0]

    row = lax.broadcasted_iota(I32, pb.shape, 0)
    prev = jnp.where(row == 0, carry_ref[...], pltpu.roll(pb, 1, axis=0))
    last = pb[tm - 1:tm, :]
    carry_ref[...] = last
    last_out[0] = last
    pm = pb + (prev - pb) * mu_ref[...]
    r = pm[:, 0:bw]
    k = pm[:, bw:2 * bw]
    v = pm[:, 2 * bw:3 * bw]
    wa = pm[:, 3 * bw:3 * bw + W_LORA + A_LORA]
    gl = pm[:, 3 * bw + W_LORA + A_LORA:]
    ones = ones_ref[...]
    w = -_softplus(-(w0_ref[...] + _dot(jnp.tanh(wa).astype(BF16), wB_ref[...]))) - 0.5
    decay = jnp.exp(-jnp.exp(w))
    a = jax.nn.sigmoid(a0_ref[...] + _dot(wa.astype(BF16), aB_ref[...]))
    g = _dot(jax.nn.sigmoid(gl).astype(BF16), gB_ref[...])
    k2 = k * (1.0 + (a - 1.0) * ka_ref[...])
    bonus = _segsum(r * k2 * rk_ref[...], ones)
    r_out[0] = r
    w_out[0] = decay
    k_out[0] = k
    a_out[0] = a
    v_out[0] = v
    g_out[0] = g
    bv_out[0] = bonus * v


def rwkv_prep(pb3, shift0, mu, w0, wB_pad, a0, aB_pad, gB, k_a, r_k, ones_bf):
    B, S, PB = pb3.shape
    bw = w0.shape[-1]
    tm = min(256, S)
    row = lambda a: a.reshape(1, -1)
    tok = pl.BlockSpec((1, tm, bw), lambda b, j: (b, j, 0))
    outs = pl.pallas_call(
        _rwkv_prep_body,
        grid=(B, S // tm),
        in_specs=[pl.BlockSpec((1, tm, PB), lambda b, j: (b, j, 0)), pl.BlockSpec((1, 1, PB), lambda b, j: (b, 0, 0)),
                  _full((1, PB)), _full((1, bw)), _full(wB_pad.shape), _full((1, bw)), _full(aB_pad.shape),
                  _full(gB.shape), _full((1, bw)), _full((1, bw)), _full(ones_bf.shape)],
        out_specs=[tok] * 7 + [pl.BlockSpec((1, 1, PB), lambda b, j: (b, 0, 0))],
        out_shape=[jax.ShapeDtypeStruct((B, S, bw), F32)] * 7 + [jax.ShapeDtypeStruct((B, 1, PB), F32)],
        scratch_shapes=[pltpu.VMEM((1, PB), F32)],
        compiler_params=_params("arbitrary", "arbitrary"), name="rwkv_prep")(
            pb3, shift0.reshape(B, 1, PB), row(mu), row(w0), wB_pad, row(a0), aB_pad, gB, row(k_a), row(r_k), ones_bf)
    return outs


def _rwkv_scan_body(r_ref, w_ref, k_ref, a_ref, v_ref, s0_ref, kkp_ref, kap_ref, y_ref, sl_ref, S_ref, op_ref, *, tc, nip):
    c = pl.program_id(1)

    @pl.when(c == 0)
    def _():
        S_ref[...] = s0_ref[0]

    dup = lambda x: jnp.concatenate([x, x], axis=1)

    def derive(t, slot):
        k = dup(k_ref[0, t])
        a = dup(a_ref[0, t])
        kk = k * kkp_ref[...]
        kk = kk * lax.rsqrt(jnp.sum(kk * kk, axis=0, keepdims=True) + 1e-12)
        op_ref[slot, 0] = dup(r_ref[0, t])
        op_ref[slot, 1] = dup(w_ref[0, t])
        op_ref[slot, 2] = k * (1.0 + (a - 1.0) * kap_ref[...])
        op_ref[slot, 3] = kk
        op_ref[slot, 4] = kk * a

    def advance(t, slot):
        for ip in range(nip):
            Sg = S_ref[ip]
            sa = jnp.sum(Sg * op_ref[slot, 3], axis=0, keepdims=True)
            vrow = v_ref[0, t, pl.ds(ip, 1), :]
            Sg = Sg * op_ref[slot, 1] - sa * op_ref[slot, 4] + vrow * op_ref[slot, 2]
            S_ref[ip] = Sg
            y_ref[0, t, pl.ds(ip, 1), :] = jnp.sum(Sg * op_ref[slot, 0], axis=0, keepdims=True)

    derive(0, 0)

    def step_pair(u, carry):
        t0 = 2 * u
        derive(t0 + 1, 1)
        advance(t0, 0)
        derive(jnp.minimum(t0 + 2, tc - 1), 0)
        advance(t0 + 1, 1)
        return carry

    assert tc % 2 == 0
    lax.fori_loop(0, tc // 2, step_pair, 0)

    @pl.when(c == pl.num_programs(1) - 1)
    def _():
        sl_ref[0] = S_ref[...]


def rwkv_scan(r, w, k, a, v, s0, kkp, kap):
    G, S, nb, hl = r.shape
    nip, L = v.shape[2:]
    tc = min(64, S)
    vec = pl.BlockSpec((1, tc, nb, hl), lambda g, c: (g, c, 0, 0))
    vsp = pl.BlockSpec((1, tc, nip, L), lambda g, c: (g, c, 0, 0))
    ssp = pl.BlockSpec((1, nip, nb, L), lambda g, c: (g, 0, 0, 0))
    return pl.pallas_call(
        functools.partial(_rwkv_scan_body, tc=tc, nip=nip),
        grid=(G, S // tc),
        in_specs=[vec] * 4 + [vsp, ssp, _full((nb, L)), _full((nb, L))],
        out_specs=[vsp, ssp],
        out_shape=[jax.ShapeDtypeStruct((G, S, nip, L), F32), jax.ShapeDtypeStruct((G, nip, nb, L), F32)],
        scratch_shapes=[pltpu.VMEM((nip, nb, L), F32), pltpu.VMEM((2, 5, nb, L), F32)],
        compiler_params=_params("arbitrary", "arbitrary"), name="rwkv_scan")(r, w, k, a, v, s0, kkp, kap)


def _to_scan(x):
    G, Bg, S, _ = x.shape
    return x.reshape(G, Bg, S, HB, NB).transpose(0, 2, 4, 1, 3).reshape(G, S, NB, Bg * HB)


def _param_to_scan(p, Bg):
    return jnp.tile(p.reshape(HB, NB).T, (1, 2 * Bg))


def _v_to_scan(x):
    G, Bg, S, _ = x.shape
    return x.reshape(G, Bg, S, HB, 2, NB // 2).transpose(0, 2, 5, 4, 1, 3).reshape(G, S, NB // 2, 2 * Bg * HB)


def _y_from_scan(y, Bg):
    G, S = y.shape[:2]
    return y.reshape(G, S, NB // 2, 2, Bg, HB).transpose(0, 4, 1, 5, 3, 2).reshape(G, Bg, S, HB * NB)


def _state_to_scan(s):
    G, Bg = s.shape[:2]
    return s.reshape(G, Bg, HB, 2, NB // 2, NB).transpose(0, 4, 5, 3, 1, 2).reshape(G, NB // 2, NB, 2 * Bg * HB)


def _state_from_scan(s, Bg):
    G = s.shape[0]
    return s.reshape(G, NB // 2, NB, 2, Bg, HB).transpose(0, 4, 5, 3, 1, 2).reshape(G, Bg, HB, NB, NB)


def _mix0_out_body(ya_ref, y_ref, bv_ref, g_ref, x_ref, gng_ref, gnb_ref, ones_ref, w_ref, o_ref):
    ones = ones_ref[...]
    y = y_ref[...]
    mean = _segsum(y, ones) * (1.0 / NB)
    d = y - mean
    var = _segsum(d * d, ones) * (1.0 / NB)
    yb = (d * lax.rsqrt(var + GN_EPS) * gng_ref[...] + gnb_ref[...] + bv_ref[...]) * g_ref[...]
    aw = ya_ref.shape[1]
    o_ref[...] = x_ref[...] + _dot(ya_ref[...].astype(BF16), w_ref[:aw, :]) + _dot(yb.astype(BF16), w_ref[aw:, :])


def mix0_out(ya, y, bv, g, x, gn_g, gn_b, ones_bf, w_bf):
    T, D = x.shape
    aw = ya.shape[1]
    bw = y.shape[1]
    tm = min(256, T)
    tok = lambda n: pl.BlockSpec((tm, n), lambda i: (i, 0))
    return pl.pallas_call(
        _mix0_out_body,
        grid=(T // tm,),
        in_specs=[tok(aw), tok(bw), tok(bw), tok(bw), tok(D), _full((1, bw)), _full((1, bw)), _full(ones_bf.shape),
                  _full(w_bf.shape)],
        out_specs=tok(D),
        out_shape=jax.ShapeDtypeStruct((T, D), F32),
        compiler_params=_params("arbitrary"), name="mix0_out")(ya, y, bv, g, x, gn_g.reshape(1, bw), gn_b.reshape(1, bw),
                                                              ones_bf, w_bf)


def _mix1_out_body(a1_ref, a2_ref, x_ref, w_ref, o_ref):
    n1 = a1_ref.shape[1]
    o_ref[...] = x_ref[...] + _dot(a1_ref[...].astype(BF16), w_ref[:n1, :]) + _dot(a2_ref[...].astype(BF16), w_ref[n1:, :])


def mix1_out(a1, a2, x, w_bf):
    T, D = x.shape
    tm = min(256, T)
    tok = lambda n: pl.BlockSpec((tm, n), lambda i: (i, 0))
    return pl.pallas_call(
        _mix1_out_body,
        grid=(T // tm,),
        in_specs=[tok(a1.shape[1]), tok(a2.shape[1]), tok(D), _full(w_bf.shape)],
        out_specs=tok(D),
        out_shape=jax.ShapeDtypeStruct((T, D), F32),
        compiler_params=_params("arbitrary"), name="mix1_out")(a1, a2, x, w_bf)


def _xattn_body(x_ref, g_ref, wq_ref, mk_ref, mv_ref, wo_ref, o_ref, *, nh):
    x = x_ref[0]
    h = _rms(x, g_ref[...]).astype(BF16)
    q = _dot(h, wq_ref[...])
    mk = mk_ref[0].astype(BF16)
    mv = mv_ref[0].astype(BF16)
    hd = q.shape[1] // nh
    outs = []
    for hh in range(nh):
        sl = slice(hh * hd, (hh + 1) * hd)
        lg = _dot_nt(q[:, sl].astype(BF16), mk[:, sl]) * (hd ** -0.5)
        e = jnp.exp(lg - jnp.max(lg, -1, keepdims=True))
        p = e / jnp.sum(e, -1, keepdims=True)
        outs.append(_dot(p.astype(BF16), mv[:, sl]))
    o = jnp.concatenate(outs, axis=1).astype(BF16)
    o_ref[0] = x + _dot(o, wo_ref[...])


def cross_attn(x3, g, wq_bf, mk, mv, wo_bf):
    B, S, D = x3.shape
    M = mk.shape[1]
    tm = min(256, S)
    return pl.pallas_call(
        functools.partial(_xattn_body, nh=XH),
        grid=(B, S // tm),
        in_specs=[pl.BlockSpec((1, tm, D), lambda b, j: (b, j, 0)), _full((1, D)), _full(wq_bf.shape),
                  pl.BlockSpec((1, M, D), lambda b, j: (b, 0, 0)), pl.BlockSpec((1, M, D), lambda b, j: (b, 0, 0)),
                  _full(wo_bf.shape)],
        out_specs=pl.BlockSpec((1, tm, D), lambda b, j: (b, j, 0)),
        out_shape=jax.ShapeDtypeStruct((B, S, D), F32),
        compiler_params=_params("arbitrary", "arbitrary"), name="cross_attn")(x3, g.reshape(1, D), wq_bf, mk, mv, wo_bf)


def _ffn_body(x_ref, g_ref, wg_ref, wu_ref, wd_ref, gf_ref, o_ref, *, final_norm):
    x = x_ref[...]
    h = _rms(x, g_ref[...]).astype(BF16)
    gate = _dot(h, wg_ref[...])
    up = _dot(h, wu_ref[...])
    act = (jax.nn.silu(gate) * up).astype(BF16)
    y = x + _dot(act, wd_ref[...])
    if final_norm:
        y = _rms(y, gf_ref[...])
    o_ref[...] = y


def ffn(x, g, wg_bf, wu_bf, wd_bf, g_final, final_norm):
    T, D = x.shape
    tm = min(256, T)
    return pl.pallas_call(
        functools.partial(_ffn_body, final_norm=final_norm),
        grid=(T // tm,),
        in_specs=[pl.BlockSpec((tm, D), lambda i: (i, 0)), _full((1, D)), _full(wg_bf.shape), _full(wu_bf.shape),
                  _full(wd_bf.shape), _full((1, D))],
        out_specs=pl.BlockSpec((tm, D), lambda i: (i, 0)),
        out_shape=jax.ShapeDtypeStruct((T, D), F32),
        compiler_params=_params("arbitrary"), name="ffn")(x, g.reshape(1, D), wg_bf, wu_bf, wd_bf, g_final.reshape(1, D))


def _shift_rows(x, d, fill):
    row = lax.broadcasted_iota(I32, x.shape, 0)
    return jnp.where(row >= d, pltpu.roll(x, d, axis=0), fill)


def _lru_body(pc_ref, h0_ref, cv0_ref, cw_ref, cb_ref, wx_ref, bx_ref, wa_ref, ba_ref, lam_ref,
              y_out, h_out, cv_out, hc_ref, cc_ref, *, reset_first):
    j = pl.program_id(1)
    pc = pc_ref[0]
    tm = pc.shape[0]
    cw = pc.shape[1] // 2
    gate_pre = pc[:, :cw]
    x = pc[:, cw:]

    @pl.when(j == 0)
    def _():
        hc_ref[...] = h0_ref[0]
        cc_ref[...] = cv0_ref[0]

    carry = cc_ref[...]
    row8 = lax.broadcasted_iota(I32, (SUBLANES, cw), 0)
    y = cb_ref[...] + x * cw_ref[CONV_W - 1:CONV_W, :]
    for d in range(1, CONV_W):
        xs = pltpu.roll(x, d, axis=0)
        first = jnp.where(row8 < d, pltpu.roll(carry, d, axis=0), xs[:SUBLANES])
        xd = first if tm == SUBLANES else jnp.concatenate([first, xs[SUBLANES:]], axis=0)
        y = y + xd * cw_ref[CONV_W - 1 - d:CONV_W - d, :]
    new_carry = x[tm - SUBLANES:, :]
    cc_ref[...] = new_carry
    cv_out[0] = new_carry

    yb = y.astype(BF16)
    gx = jax.nn.sigmoid(_dot(yb, wx_ref[...]) + bx_ref[...])
    ga = jax.nn.sigmoid(_dot(yb, wa_ref[...]) + ba_ref[...])
    log_a = LRU_C * ga * (-_softplus(-lam_ref[...]))
    a = jnp.exp(log_a)
    th = jnp.tanh(log_a)
    mult = jnp.sqrt(-2.0 * th / (1.0 - th))
    if reset_first:
        row = lax.broadcasted_iota(I32, (tm, cw), 0)
        mult = jnp.where((row == 0) & (j == 0), 1.0, mult)
    bt = y * gx * mult
    d = 1
    while d < tm:
        bt = a * _shift_rows(bt, d, 0.0) + bt
        a = a * _shift_rows(a, d, 1.0)
        d *= 2
    h = bt + a * hc_ref[...]
    h_last = h[tm - 1:tm, :]
    hc_ref[...] = h_last
    h_out[0] = h_last
    y_out[0] = h * jax.nn.gelu(gate_pre)


def lru_mixer(pc3, h0, conv0, conv_w, conv_b, wx_bd, bx, wa_bd, ba, lam, reset_first):
    B, S, W = pc3.shape
    cw = W // 2
    tm = min(256, S)
    cv0 = jnp.concatenate([jnp.zeros((B, SUBLANES - (CONV_W - 1), cw), F32), conv0], axis=1)
    row = lambda a: a.reshape(1, cw)
    y, h_last, cv = pl.pallas_call(
        functools.partial(_lru_body, reset_first=reset_first),
        grid=(B, S // tm),
        in_specs=[pl.BlockSpec((1, tm, W), lambda b, j: (b, j, 0)), pl.BlockSpec((1, 1, cw), lambda b, j: (b, 0, 0)),
                  pl.BlockSpec((1, SUBLANES, cw), lambda b, j: (b, 0, 0)), _full((CONV_W, cw)), _full((1, cw)),
                  _full(wx_bd.shape), _full((1, cw)), _full(wa_bd.shape), _full((1, cw)), _full((1, cw))],
        out_specs=[pl.BlockSpec((1, tm, cw), lambda b, j: (b, j, 0)), pl.BlockSpec((1, 1, cw), lambda b, j: (b, 0, 0)),
                   pl.BlockSpec((1, SUBLANES, cw), lambda b, j: (b, 0, 0))],
        out_shape=[jax.ShapeDtypeStruct((B, S, cw), F32), jax.ShapeDtypeStruct((B, 1, cw), F32),
                   jax.ShapeDtypeStruct((B, SUBLANES, cw), F32)],
        scratch_shapes=[pltpu.VMEM((1, cw), F32), pltpu.VMEM((SUBLANES, cw), F32)],
        compiler_params=_params("arbitrary", "arbitrary"), name="lru_mixer")(
            pc3, h0.reshape(B, 1, cw), cv0, conv_w, row(conv_b), wx_bd, row(bx), wa_bd, row(ba), row(lam))
    return y, h_last[:, 0], cv[:, SUBLANES - (CONV_W - 1):]


def _rope_tables(pos, head_dim):
    rot = head_dim // 4
    half = rot // 2
    inv = ROPE_THETA ** (-jnp.arange(half, dtype=F32) * 2.0 / rot)
    ang = pos.astype(F32)[:, None] * inv[None, :]
    cos, sin = jnp.cos(ang), jnp.sin(ang)
    S = pos.shape[0]
    rest = head_dim - rot
    c = jnp.concatenate([cos, cos, jnp.ones((S, rest), F32)], -1)
    sa = jnp.concatenate([-sin, jnp.zeros((S, head_dim - half), F32)], -1)
    sb = jnp.concatenate([jnp.zeros((S, half), F32), sin, jnp.zeros((S, rest), F32)], -1)
    rep = LANES // head_dim
    return tuple(jnp.tile(t, (1, rep)) for t in (c, sa, sb))


def _rope(x, c, sa, sb, half):
    n = x.shape[1] // LANES
    tile = (lambda t: t) if n == 1 else (lambda t: jnp.concatenate([t] * n, axis=1))
    w = x.shape[1]
    return x * tile(c) + pltpu.roll(x, w - half, axis=1) * tile(sa) + pltpu.roll(x, half, axis=1) * tile(sb)


def _rope_all(q_ref, k_ref, qi_ref, kiw_ref, cq_ref, saq_ref, sbq_ref, ci_ref, sai_ref, sbi_ref):
    qr = _rope(q_ref[...], cq_ref[...], saq_ref[...], sbq_ref[...], HD // 8)
    kr = _rope(k_ref[...], cq_ref[...], saq_ref[...], sbq_ref[...], HD // 8)
    qir = _rope(qi_ref[...], ci_ref[...], sai_ref[...], sbi_ref[...], D_I // 8)
    kiw = kiw_ref[...]
    lane = lax.broadcasted_iota(I32, kiw.shape, 1)
    iski = lane < D_I
    kiwr = _rope(kiw, jnp.where(iski, ci_ref[...], 1.0), jnp.where(iski, sai_ref[...], 0.0),
                 jnp.where(iski, sbi_ref[...], 0.0), D_I // 8)
    return qr, kr, qir, kiwr


def _dsa_prep_sample_body(q_ref, k_ref, qi_ref, kiw_ref, cq_ref, saq_ref, sbq_ref, ci_ref, sai_ref, sbi_ref,
                          qr_out, kr_out, qir_out, kiwr_out):
    qr, kr, qir, kiwr = _rope_all(q_ref, k_ref, qi_ref, kiw_ref, cq_ref, saq_ref, sbq_ref, ci_ref, sai_ref, sbi_ref)
    qr_out[...] = qr
    kr_out[...] = kr
    qir_out[...] = qir
    kiwr_out[...] = kiwr


def _dsa_prep_prompt_body(q_ref, k_ref, v_ref, qi_ref, kiw_ref, cq_ref, saq_ref, sbq_ref, ci_ref, sai_ref, sbi_ref,
                          kr_out, kiwr_out, wq_out, wi_out, wrow_out, kbf_out, kib_out, vt_out):
    qr, kr, qir, kiwr = _rope_all(q_ref, k_ref, qi_ref, kiw_ref, cq_ref, saq_ref, sbq_ref, ci_ref, sai_ref, sbi_ref)
    tm = qr.shape[0]
    kr_out[...] = kr
    kiwr_out[...] = kiwr
    kbf_out[...] = kr.astype(BF16)
    kib_out[...] = kiwr.astype(BF16)
    vt_out[0] = v_ref[...].T.astype(BF16)
    qt = (qr * (HD ** -0.5 * LOG2E)).T
    z = jnp.zeros((HD, tm), F32)
    slabs = []
    for h in range(H_D):
        piece = qt[h * HD:(h + 1) * HD, :]
        slabs.append(jnp.concatenate([piece, z] if h < H_D // N_KV else [z, piece], axis=0))
    wq_out[0] = jnp.concatenate(slabs, axis=1).astype(BF16)
    qit = qir.T
    zi = jnp.zeros((LANES - D_I, tm), F32)
    wi_out[0] = jnp.concatenate([jnp.concatenate([qit[h * D_I:(h + 1) * D_I, :], zi], axis=0) for h in range(H_I)],
                                axis=1).astype(BF16)
    wrow_out[0] = kiwr.T[D_I:D_I + H_I, :] * (H_I ** -0.5 * D_I ** -0.5)


def dsa_prep(q, k, v, qi, kiw, tabs_q, tabs_i, prompt):
    T = q.shape[0]
    tm = Q_BLOCK if prompt else min(256, T)
    tok = lambda n: pl.BlockSpec((tm, n), lambda i: (i, 0))
    nq, nk, ni = q.shape[1], k.shape[1], qi.shape[1]
    tabs = list(tabs_q) + list(tabs_i)
    if not prompt:
        return pl.pallas_call(
            _dsa_prep_sample_body,
            grid=(T // tm,),
            in_specs=[tok(nq), tok(nk), tok(ni), tok(LANES)] + [tok(LANES)] * 6,
            out_specs=[tok(nq), tok(nk), tok(ni), tok(LANES)],
            out_shape=[jax.ShapeDtypeStruct((T, n), F32) for n in (nq, nk, ni, LANES)],
            compiler_params=_params("arbitrary"), name="dsa_prep_sample")(q, k, qi, kiw, *tabs)
    nblk = T // tm
    blk = lambda r, c: pl.BlockSpec((1, r, c), lambda i: (i, 0, 0))
    nper = tabs[0].shape[0] // tm
    return pl.pallas_call(
        _dsa_prep_prompt_body,
        grid=(nblk,),
        in_specs=[tok(nq), tok(nk), tok(nk), tok(ni), tok(LANES)] + [pl.BlockSpec((tm, LANES), lambda i: (i % nper, 0))] * 6,
        out_specs=[tok(nk), tok(LANES), blk(nk, H_D * tm), blk(LANES, H_I * tm), blk(H_I, tm), tok(nk), tok(LANES),
                   blk(nk, tm)],
        out_shape=[jax.ShapeDtypeStruct((T, nk), F32), jax.ShapeDtypeStruct((T, LANES), F32),
                   jax.ShapeDtypeStruct((nblk, nk, H_D * tm), BF16), jax.ShapeDtypeStruct((nblk, LANES, H_I * tm), BF16),
                   jax.ShapeDtypeStruct((nblk, H_I, tm), F32), jax.ShapeDtypeStruct((T, nk), BF16),
                   jax.ShapeDtypeStruct((T, LANES), BF16), jax.ShapeDtypeStruct((nblk, nk, tm), BF16)],
        compiler_params=_params("arbitrary"), name="dsa_prep_prompt")(q, k, v, qi, kiw, *tabs)


def _key_to_f32(k):
    return pltpu.bitcast(k ^ ((k >> 31) & 0x7FFFFFFF), F32)


def _kth_largest(count_ge, shape, k):
    def step(it, key):
        cand = key + jnp.left_shift(jnp.int32(1), 31 - it)
        return jnp.where(count_ge(_key_to_f32(cand)) >= k, cand, key)

    key = lax.fori_loop(0, 32, step, jnp.full(shape, INT_MIN, I32))
    return jnp.where(key <= KEY_NEG_INF, -jnp.inf, _key_to_f32(key))


def _dsa_prompt_body(wq_ref, wi_ref, wrow_ref, kbf_ref, kib_ref, vt_ref, o_ref,
                     sc_ref, acc_ref, bias_ref, *, topk, kbs):
    i = pl.program_id(1)
    qb = Q_BLOCK
    q0 = i * qb
    ls = kbs * qb
    la = ls // 2
    ns = (q0 + qb + ls - 1) // ls
    sub = lax.broadcasted_iota(I32, (qb, qb), 0)
    lane = lax.broadcasted_iota(I32, (qb, qb), 1)
    wi = wi_ref[0]
    wrow = wrow_ref[0]

    def score_block(kb, carry):
        k0 = kb * ls
        r = _dot(kib_ref[0, pl.ds(k0, ls), :], wi)
        for c in range(kbs):
            rc = r[c * qb:(c + 1) * qb, :]
            acc = wrow[0:1, :] * jnp.maximum(rc[:, 0:qb], 0.0)
            for h in range(1, H_I):
                acc = acc + wrow[h:h + 1, :] * jnp.maximum(rc[:, h * qb:(h + 1) * qb], 0.0)
            causal = (k0 + c * qb + sub) <= (q0 + lane)
            sc_ref[pl.ds(k0 + c * qb, qb), :] = jnp.where(causal, acc, -jnp.inf)
        return carry

    lax.fori_loop(0, ns, score_block, 0)

    def count(pred):
        def body(kb, cnt):
            for c in range(kbs):
                m = jnp.where(pred(sc_ref[pl.ds(kb * ls + c * qb, qb), :]), 1, 0)
                cnt = cnt + jnp.sum(m.reshape(qb // SUBLANES, SUBLANES, qb), axis=0)
            return cnt
        cnt = lax.fori_loop(0, ns, body, jnp.zeros((SUBLANES, qb), I32))
        return jnp.sum(cnt, axis=0, keepdims=True)

    thr = _kth_largest(lambda t: count(lambda s: s >= t), (1, qb), topk)
    need = (topk - count(lambda s: s > thr)).astype(F32)
    excess = (count(lambda s: s >= thr) > topk) & (thr > -jnp.inf)

    @pl.when(jnp.max(jnp.where(excess, 1, 0)) > 0)
    def _():
        ltri = jnp.where(lane < sub, 1.0, 0.0).astype(BF16)

        def body(c, run):
            s = sc_ref[pl.ds(c * qb, qb), :]
            eq = s == thr
            eqf = jnp.where(eq, 1.0, 0.0)
            pre = _dot(ltri, eqf.astype(BF16)) + run
            sc_ref[pl.ds(c * qb, qb), :] = jnp.where(eq & (pre >= need), -jnp.inf, s)
            return run + jnp.sum(eqf, axis=0, keepdims=True)

        lax.fori_loop(0, ns * kbs, body, jnp.zeros((1, qb), F32))

    acc_ref[...] = jnp.zeros(acc_ref.shape, F32)
    hpg = H_D // N_KV

    def attend(kb, carry):
        ms, ls_ = carry
        logits = _dot(kbf_ref[0, pl.ds(kb * la, la), :], wq_ref[0])
        s = sc_ref[pl.ds(kb * la, la), :]
        bias_ref[...] = jnp.where((s >= thr) & (s > -jnp.inf), 0.0, NEG)
        vt = vt_ref[0, kb]
        ms_new, ls_new = [], []
        for g in range(N_KV):
            ps, alphas = [], []
            for h in range(g * hpg, (g + 1) * hpg):
                lg = logits[:, h * qb:(h + 1) * qb] + bias_ref[...]
                m_new = jnp.maximum(ms[h], jnp.max(lg, axis=0, keepdims=True))
                p = jnp.exp2(lg - m_new)
                alpha = jnp.exp2(ms[h] - m_new)
                ms_new.append(m_new)
                ls_new.append(alpha * ls_[h] + jnp.sum(p, axis=0, keepdims=True))
                ps.append(p.astype(BF16))
                alphas.append(alpha)
            gs = slice(g * hpg * qb, (g + 1) * hpg * qb)
            acc_ref[:, gs] = (acc_ref[:, gs] * jnp.concatenate(alphas, axis=1)
                              + _dot(vt[g * HD:(g + 1) * HD, :], jnp.concatenate(ps, axis=1)))
        return tuple(ms_new), tuple(ls_new)

    init = (tuple(jnp.full((1, qb), NEG, F32) for _ in range(H_D)), tuple(jnp.zeros((1, qb), F32) for _ in range(H_D)))
    _, l_fin = lax.fori_loop(0, (q0 + qb + la - 1) // la, attend, init)
    o_ref[0] = jnp.concatenate([acc_ref[:, h * qb:(h + 1) * qb] * (1.0 / l_fin[h]) for h in range(H_D)], axis=0).T


def dsa_prompt(wq, wi, wrow, kbf, kib, vt, B, S):
    nq = S // Q_BLOCK
    topk = min(TOPK_MAX, S // 4)
    assert nq % 2 == 0
    kbs = 4 if nq % 4 == 0 else 2
    kba = kbs // 2
    la = kba * Q_BLOCK
    nk = kbf.shape[-1]
    vt4 = vt.reshape(B, S // la, kba, nk, Q_BLOCK).transpose(0, 1, 3, 2, 4).reshape(B, S // la, nk, la)
    qspec = lambda r, c: pl.BlockSpec((1, r, c), lambda b, i: (b * nq + i, 0, 0))
    return pl.pallas_call(
        functools.partial(_dsa_prompt_body, topk=topk, kbs=kbs),
        grid=(B, nq),
        in_specs=[qspec(nk, H_D * Q_BLOCK), qspec(LANES, H_I * Q_BLOCK), qspec(H_I, Q_BLOCK),
                  pl.BlockSpec((1, S, nk), lambda b, i: (b, 0, 0)), pl.BlockSpec((1, S, LANES), lambda b, i: (b, 0, 0)),
                  pl.BlockSpec((1, S // la, nk, la), lambda b, i: (b, 0, 0, 0))],
        out_specs=pl.BlockSpec((1, Q_BLOCK, H_D * HD), lambda b, i: (b, i, 0)),
        out_shape=jax.ShapeDtypeStruct((B, S, H_D * HD), F32),
        scratch_shapes=[pltpu.VMEM((S, Q_BLOCK), F32), pltpu.VMEM((HD, H_D * Q_BLOCK), F32),
                        pltpu.VMEM((la, Q_BLOCK), F32)],
        compiler_params=_params("arbitrary", "arbitrary"), name="dsa_prompt")(
            wq, wi, wrow, kbf.reshape(B, S, nk), kib.reshape(B, S, LANES), vt4)


def _page_copies(pt_ref, pool_ref, buf_ref, sem_ref, step, slot, pps):
    return [pltpu.make_async_copy(pool_ref.at[pt_ref[step * pps + i]], buf_ref.at[slot, i], sem_ref.at[slot])
            for i in range(pps)]


def _index_scores(qi, wb, pages_bf, t):
    r = _dot(qi, pages_bf)
    acc = wb[0:t, :] * jnp.maximum(r[0:t, :], 0.0)
    for h in range(1, H_I):
        acc = acc + wb[h * t:(h + 1) * t, :] * jnp.maximum(r[h * t:(h + 1) * t, :], 0.0)
    return acc


def _dsa_s_scores_body(pt_ref, qi_ref, wb_ref, new_ref, pool_ref, s_ref, snew_ref, buf_ref, sem_ref, *, pps, nc, t):
    s = pl.program_id(0)
    slot = s % 2
    copies = functools.partial(_page_copies, pt_ref, pool_ref, buf_ref, sem_ref, pps=pps)

    @pl.when(s == 0)
    def _():
        for cp in copies(s, slot):
            cp.start()

    @pl.when(s + 1 < pl.num_programs(0))
    def _():
        for cp in copies(s + 1, 1 - slot):
            cp.start()

    qi = qi_ref[0]
    wb = wb_ref[0]

    @pl.when(s % nc == 0)
    def _():
        acc = _index_scores(qi, wb, new_ref[0].astype(BF16), t)
        sub = lax.broadcasted_iota(I32, acc.shape, 0)
        lane = lax.broadcasted_iota(I32, acc.shape, 1)
        snew_ref[0] = jnp.where(lane <= sub, acc, -jnp.inf)

    for cp in copies(s, slot):
        cp.wait()
    pages = buf_ref[slot]
    for i in range(pps):
        s_ref[0, i] = _index_scores(qi, wb, pages[i].astype(BF16), t)


def _dsa_s_attend_body(pt_ref, qm_ref, sp_ref, sn_ref, knew_ref, vnew_ref, kpool_ref, vpool_ref, o_ref,
                       kbuf_ref, vbuf_ref, sem_ref, sc_ref, thr_ref, m_ref, l_ref, acc_ref, *, pps, nc, t, topk):
    s = pl.program_id(0)
    c = s % nc
    slot = s % 2
    npg = nc * pps
    kcopies = functools.partial(_page_copies, pt_ref, kpool_ref, kbuf_ref, sem_ref.at[0], pps=pps)
    vcopies = functools.partial(_page_copies, pt_ref, vpool_ref, vbuf_ref, sem_ref.at[1], pps=pps)

    @pl.when(s == 0)
    def _():
        for cp in kcopies(s, slot) + vcopies(s, slot):
            cp.start()

    @pl.when(s + 1 < pl.num_programs(0))
    def _():
        for cp in kcopies(s + 1, 1 - slot) + vcopies(s + 1, 1 - slot):
            cp.start()

    @pl.when(c == 0)
    def _():
        sc_ref[0:npg] = sp_ref[0]
        sc_ref[npg] = sn_ref[0]

        def count(pred):
            cnt = jnp.zeros((t, PAGE_SIZE), I32)
            for ch in range(npg + 1):
                cnt = cnt + jnp.where(pred(sc_ref[ch]), 1, 0)
            return jnp.sum(cnt, axis=1, keepdims=True)

        thr = _kth_largest(lambda x: count(lambda sc: sc >= x), (t, 1), topk)
        need = (topk - count(lambda sc: sc > thr)).astype(F32)
        excess = (count(lambda sc: sc >= thr) > topk) & (thr > -jnp.inf)

        @pl.when(jnp.max(jnp.where(excess, 1, 0)) > 0)
        def _():
            sub = lax.broadcasted_iota(I32, (PAGE_SIZE, PAGE_SIZE), 0)
            lane = lax.broadcasted_iota(I32, (PAGE_SIZE, PAGE_SIZE), 1)
            utri = jnp.where(sub < lane, 1.0, 0.0).astype(BF16)

            def body(ch, run):
                sc = sc_ref[ch]
                eq = sc == thr
                eqf = jnp.where(eq, 1.0, 0.0)
                pre = _dot(eqf.astype(BF16), utri) + run
                sc_ref[ch] = jnp.where(eq & (pre >= need), -jnp.inf, sc)
                return run + jnp.sum(eqf, axis=1, keepdims=True)

            lax.fori_loop(0, npg + 1, body, jnp.zeros((t, 1), F32))

        thr_ref[...] = jnp.broadcast_to(thr, thr_ref.shape)
        m_ref[...] = jnp.full(m_ref.shape, NEG, F32)
        l_ref[...] = jnp.zeros(l_ref.shape, F32)
        acc_ref[...] = jnp.zeros(acc_ref.shape, F32)

    qm = qm_ref[0]
    thr = thr_ref[...]

    def bias_rows(sc):
        b = jnp.where((sc >= thr) & (sc > -jnp.inf), 0.0, NEG)
        return jnp.concatenate([b] * H_D, axis=0)

    def update(k_bf, v_bf, bias):
        lg = _dot(qm, k_bf) + bias
        m_old = m_ref[...]
        m_new = jnp.maximum(m_old, jnp.max(lg, axis=1, keepdims=True))
        p = jnp.exp2(lg - m_new)
        alpha = jnp.exp2(m_old - m_new)
        m_ref[...] = m_new
        l_ref[...] = alpha * l_ref[...] + jnp.sum(p, axis=1, keepdims=True)
        acc_ref[...] = alpha * acc_ref[...] + _dot_nt(p.astype(BF16), v_bf)

    for cp in kcopies(s, slot) + vcopies(s, slot):
        cp.wait()
    bias = jnp.concatenate([bias_rows(sc_ref[c * pps + i]) for i in range(pps)], axis=1)
    pages = lambda buf: jnp.concatenate([buf[slot, i].astype(BF16) for i in range(pps)], axis=1)
    update(pages(kbuf_ref), pages(vbuf_ref), bias)

    @pl.when(c == nc - 1)
    def _():
        update(knew_ref[0].astype(BF16), vnew_ref[0].astype(BF16), bias_rows(sc_ref[npg]))
        o_ref[0] = acc_ref[...] / l_ref[...]


def dsa_sample(qr, kr, v, qir, kiwr, pool_k, pool_v, pool_ki, page_table, B, T):
    npg = page_table.shape[1]
    npool = pool_k.shape[0]
    nkv = N_KV * HD
    topk = min(TOPK_MAX, (npg * PAGE_SIZE + T) // 4)
    pps = 32 if npg % 32 == 0 else npg
    nc = npg // pps
    page_t = lambda pool: jnp.swapaxes(pool.reshape(npool, PAGE_SIZE, -1), 1, 2)
    pad_new = lambda a: jnp.swapaxes(jnp.pad(a.reshape(B, T, -1), ((0, 0), (0, PAGE_SIZE - T), (0, 0))), 1, 2)
    ki_new = pad_new(kiwr[:, :D_I])
    k_new = pad_new(kr)
    v_new = pad_new(v)
    qi_m = qir.reshape(B, T, H_I, D_I).transpose(0, 2, 1, 3).reshape(B, H_I * T, D_I).astype(BF16)
    wi = kiwr[:, D_I:D_I + H_I].reshape(B, T, H_I) * (H_I ** -0.5 * D_I ** -0.5)
    wb = jnp.broadcast_to(wi.transpose(0, 2, 1).reshape(B, H_I * T, 1), (B, H_I * T, PAGE_SIZE))
    pt_flat = page_table.reshape(-1)
    per_b = lambda r, c: pl.BlockSpec((1, r, c), lambda s, pt: (s // nc, 0, 0))
    hbm = pl.BlockSpec(memory_space=pl.ANY)
    sc_past, sc_new = pl.pallas_call(
        functools.partial(_dsa_s_scores_body, pps=pps, nc=nc, t=T),
        grid_spec=pltpu.PrefetchScalarGridSpec(
            num_scalar_prefetch=1, grid=(B * nc,),
            in_specs=[per_b(H_I * T, D_I), per_b(H_I * T, PAGE_SIZE), per_b(D_I, PAGE_SIZE), hbm],
            out_specs=[pl.BlockSpec((1, pps, T, PAGE_SIZE), lambda s, pt: (s // nc, s % nc, 0, 0)), per_b(T, PAGE_SIZE)],
            scratch_shapes=[pltpu.VMEM((2, pps, D_I, PAGE_SIZE), F32), pltpu.SemaphoreType.DMA((2,))]),
        out_shape=[jax.ShapeDtypeStruct((B, npg, T, PAGE_SIZE), F32), jax.ShapeDtypeStruct((B, T, PAGE_SIZE), F32)],
        compiler_params=_params("arbitrary"), name="dsa_sample_scores")(pt_flat, qi_m, wb, ki_new, page_t(pool_ki))
    q4 = (qr * (HD ** -0.5 * LOG2E)).reshape(B, T, N_KV, H_D // N_KV, HD).transpose(0, 2, 3, 1, 4)
    eye = jnp.eye(N_KV, dtype=F32)
    qm = (q4[:, :, :, :, None, :] * eye[None, :, None, None, :, None]).reshape(B, H_D * T, nkv).astype(BF16)
    out = pl.pallas_call(
        functools.partial(_dsa_s_attend_body, pps=pps, nc=nc, t=T, topk=topk),
        grid_spec=pltpu.PrefetchScalarGridSpec(
            num_scalar_prefetch=1, grid=(B * nc,),
            in_specs=[per_b(H_D * T, nkv), pl.BlockSpec((1, npg, T, PAGE_SIZE), lambda s, pt: (s // nc, 0, 0, 0)),
                      per_b(T, PAGE_SIZE), per_b(nkv, PAGE_SIZE), per_b(nkv, PAGE_SIZE), hbm, hbm],
            out_specs=per_b(H_D * T, nkv),
            scratch_shapes=[pltpu.VMEM((2, pps, nkv, PAGE_SIZE), F32), pltpu.VMEM((2, pps, nkv, PAGE_SIZE), F32),
                            pltpu.SemaphoreType.DMA((2, 2)), pltpu.VMEM((npg + 1, T, PAGE_SIZE), F32),
                            pltpu.VMEM((T, PAGE_SIZE), F32), pltpu.VMEM((H_D * T, 1), F32),
                            pltpu.VMEM((H_D * T, 1), F32), pltpu.VMEM((H_D * T, nkv), F32)]),
        out_shape=jax.ShapeDtypeStruct((B, H_D * T, nkv), F32),
        compiler_params=_params("arbitrary"), name="dsa_sample_attend")(
            pt_flat, qm, sc_past, sc_new, k_new, v_new, page_t(pool_k), page_t(pool_v))
    o5 = out.reshape(B, N_KV, H_D // N_KV, T, N_KV, HD)
    og = jnp.stack([o5[:, g, :, :, g, :] for g in range(N_KV)], axis=1)
    return og.transpose(0, 3, 1, 2, 4).reshape(B * T, H_D * HD)


def _block_diag(w):
    n, a, b = w.shape
    eye = jnp.eye(n, dtype=w.dtype)
    return (w[:, :, None, :] * eye[:, None, :, None]).reshape(n * a, n * b)


def _prep_weights(P):
    bf = lambda a: a.astype(BF16)
    bw = HB * NB
    pb_w = 3 * bw + W_LORA + A_LORA + G_LORA
    perm = np.concatenate([np.arange(0, bw), np.arange(bw + W_LORA, 3 * bw + W_LORA), np.arange(bw, bw + W_LORA),
                           np.arange(3 * bw + W_LORA, pb_w)])
    W = {"perm": perm, "inv_perm": np.argsort(perm)}
    aw2 = P["e_w_in"].shape[-1] - pb_w
    W["e_w_in"] = [bf(jnp.concatenate([w[:, :aw2], w[:, aw2:][:, perm]], axis=1)) for w in P["e_w_in"]]
    W["b_mu"] = [m[perm] for m in P["b_mu"]]
    W["wB_pad"] = [bf(jnp.concatenate([w, jnp.zeros((A_LORA, bw), F32)], 0)) for w in P["b_wB"]]
    W["aB_pad"] = [bf(jnp.concatenate([jnp.zeros((W_LORA, bw), F32), w], 0)) for w in P["b_aB"]]
    W["gB"] = [bf(w) for w in P["b_gB"]]
    W["ones_bd"] = bf(_block_diag(jnp.ones((HB, NB, NB), F32)))
    o_in = []
    for w in P["o_w_in"]:
        n_main = w.shape[1] - (D_I + H_I)
        o_in.append(bf(jnp.concatenate([w, jnp.zeros((w.shape[0], LANES - (D_I + H_I)), F32)], axis=1)))
        assert n_main % LANES == 0
    W["o_w_in"] = o_in
    W["wx_bd"] = [bf(_block_diag(w)) for w in P["c_wx"]]
    W["wa_bd"] = [bf(_block_diag(w)) for w in P["c_wa"]]
    for name in ("e_w_out", "o_w_out", "w_xq", "w_xo", "w_ff_gate", "w_ff_up", "w_ff_down"):
        W[name] = [bf(w) for w in P[name]]
    W["w_xkv"] = [bf(jnp.concatenate([k, v], axis=1)) for k, v in zip(P["w_xk"], P["w_xv"])]
    return W


def _memory_kv(mem, norm_mem, W):
    B, M, D = mem.shape
    ks, vs = [], []
    for layer in range(len(W["w_xkv"])):
        k, v = rms_matmul(mem.reshape(B * M, D), norm_mem[layer], W["w_xkv"][layer], (D, D))
        ks.append(k.reshape(B, M, D))
        vs.append(v.reshape(B, M, D))
    return ks, vs


def _trunk(x3, pos0, rwkv_s0, rwkv_shift0, lru_h0, lru_conv0, mem_k, mem_v, P, W, sample_ctx):
    B, S, D = x3.shape
    T = B * S
    prompt = sample_ctx is None
    depth = P["norm_mix"].shape[0]
    x = x3.reshape(T, D)
    pos = pos0 + jnp.arange(S, dtype=I32)
    out = {k: [] for k in ("chunk_v", "rwkv_s", "rwkv_sh", "lru_h", "lru_cv", "att_k", "att_v", "att_ki")}
    bw = HB * NB
    for layer in range(depth):
        j = layer // 2
        if layer % 2 == 0:
            aw2 = W["e_w_in"][j].shape[1] - (3 * bw + W_LORA + A_LORA + G_LORA)
            pa, pb = rms_matmul(x, P["norm_mix"][layer], W["e_w_in"][j], (aw2, W["e_w_in"][j].shape[1] - aw2))
            tril = jnp.tril(P["a_ws"][j])
            if S % CHUNK == 0:
                mix, bias = tril, P["a_bs"][j]
            else:
                assert S <= CHUNK
                eye = jnp.eye(B, dtype=F32)
                mix = (tril[:, None, :S, None, :S] * eye[None, :, None, :, None]).reshape(A_G, T, T)
                bias = jnp.tile(P["a_bs"][j][:, :S], (1, B))
            ac = aw2 // 2 // A_G
            bias_full = jnp.broadcast_to(bias[:, :, None], bias.shape + (ac,))
            mixed = chunk_mixer(pa, P["a_ln_g"][j], P["a_ln_b"][j], mix.astype(BF16), bias_full, emit_v=not prompt)
            ya = mixed[0]
            pbw = pb.shape[1]
            sh0 = rwkv_shift0[j][:, W["perm"]]
            r, dec, k, a, v, g, bv, last = rwkv_prep(
                pb.reshape(B, S, pbw), sh0, W["b_mu"][j], P["b_w0"][j], W["wB_pad"][j], P["b_a0"][j], W["aB_pad"][j],
                W["gB"][j], P["b_ka"][j], P["b_rk"][j].reshape(-1), W["ones_bd"])
            Bg = LANES // (2 * HB)
            G = B // Bg
            grp = lambda t: t.reshape(G, Bg, S, bw)
            y_s, s_last = rwkv_scan(_to_scan(grp(r)), _to_scan(grp(dec)), _to_scan(grp(k)), _to_scan(grp(a)),
                                    _v_to_scan(grp(v)), _state_to_scan(rwkv_s0[j].astype(F32).reshape(G, Bg, HB, NB, NB)),
                                    _param_to_scan(P["b_kk"][j], Bg), _param_to_scan(P["b_ka"][j], Bg))
            y = _y_from_scan(y_s, Bg).reshape(T, bw)
            x = mix0_out(ya, y, bv.reshape(T, bw), g.reshape(T, bw), x, P["b_gn_g"][j], P["b_gn_b"][j], W["ones_bd"],
                         W["e_w_out"][j])
            out["chunk_v"].append(mixed[-1].reshape(B, S, -1))
            out["rwkv_s"].append(_state_from_scan(s_last, Bg).reshape(B, HB, NB, NB))
            out["rwkv_sh"].append(last[:, 0][:, W["inv_perm"]])
        else:
            cw2 = 2 * P["c_conv_b"].shape[-1]
            nq, nk, ni = H_D * HD, N_KV * HD, H_I * D_I
            pc, q, k, v, qi, kiw = rms_matmul(x, P["norm_mix"][layer], W["o_w_in"][j], (cw2, nq, nk, nk, ni, LANES))
            yc, h_last, conv_new = lru_mixer(pc.reshape(B, S, cw2), lru_h0[j], lru_conv0[j], P["c_conv_w"][j],
                                             P["c_conv_b"][j], W["wx_bd"][j], P["c_bx"][j], W["wa_bd"][j], P["c_ba"][j],
                                             P["c_lambda"][j], reset_first=prompt)
            tabs_q, tabs_i = _rope_tables(pos, HD), _rope_tables(pos, D_I)
            if not prompt:
                tabs_q, tabs_i = (tuple(jnp.tile(t, (B, 1)) for t in tabs) for tabs in (tabs_q, tabs_i))
            if prompt:
                kr, kiwr, wq, wi, wrow, kbf, kib, vt = dsa_prep(q, k, v, qi, kiw, tabs_q, tabs_i, True)
                yd = dsa_prompt(wq, wi, wrow, kbf, kib, vt, B, S).reshape(T, nq)
            else:
                qr, kr, qir, kiwr = dsa_prep(q, k, v, qi, kiw, tabs_q, tabs_i, False)
                pool_k, pool_v, pool_ki, page_table = sample_ctx
                yd = dsa_sample(qr, kr, v, qir, kiwr, pool_k[j], pool_v[j], pool_ki[j], page_table, B, S)
            x = mix1_out(yc.reshape(T, -1), yd, x, W["o_w_out"][j])
            out["lru_h"].append(h_last)
            out["lru_cv"].append(conv_new)
            out["att_k"].append(kr.reshape(B, S, N_KV, HD))
            out["att_v"].append(v.reshape(B, S, N_KV, HD))
            out["att_ki"].append(kiwr[:, :D_I].reshape(B, S, D_I))
        M = mem_k[layer].shape[1]
        x = cross_attn(x.reshape(B, S, D), P["norm_x"][layer], W["w_xq"][layer], mem_k[layer].reshape(B, M, D),
                       mem_v[layer].reshape(B, M, D), W["w_xo"][layer]).reshape(T, D)
        x = ffn(x, P["norm_ffn"][layer], W["w_ff_gate"][layer], W["w_ff_up"][layer], W["w_ff_down"][layer],
                P["norm_final"], final_norm=(layer == depth - 1))
    st = lambda l: jnp.stack(l, 0)
    return (x.reshape(B, S, D),) + tuple(st(out[k]) for k in
                                         ("chunk_v", "rwkv_s", "rwkv_sh", "lru_h", "lru_cv", "att_k", "att_v", "att_ki"))


def kernel(x_prompt, x_sample, mem_prompt, state_rwkv, state_rwkv_shift, state_lru_h, state_lru_conv, cache_attn_k, cache_attn_v, cache_attn_kidx, cache_mem_k, cache_mem_v, page_table, norm_mix, norm_x, norm_mem, norm_ffn, norm_final, w_xq, w_xk, w_xv, w_xo, w_ff_gate, w_ff_up, w_ff_down, e_w_in, e_w_out, a_ln_g, a_ln_b, a_ws, a_bs, b_mu, b_w0, b_wB, b_a0, b_aB, b_gB, b_kk, b_ka, b_rk, b_gn_g, b_gn_b, o_w_in, o_w_out, c_conv_w, c_conv_b, c_wx, c_bx, c_wa, c_ba, c_lambda):
    P = dict(norm_mix=norm_mix, norm_x=norm_x, norm_ffn=norm_ffn, norm_final=norm_final,
             w_xq=w_xq, w_xk=w_xk, w_xv=w_xv, w_xo=w_xo, w_ff_gate=w_ff_gate, w_ff_up=w_ff_up, w_ff_down=w_ff_down,
             e_w_in=e_w_in, e_w_out=e_w_out, a_ln_g=a_ln_g, a_ln_b=a_ln_b, a_ws=a_ws, a_bs=a_bs,
             b_mu=b_mu, b_w0=b_w0, b_wB=b_wB, b_a0=b_a0, b_aB=b_aB, b_gB=b_gB, b_kk=b_kk, b_ka=b_ka,
             b_rk=b_rk, b_gn_g=b_gn_g, b_gn_b=b_gn_b,
             o_w_in=o_w_in, o_w_out=o_w_out, c_conv_w=c_conv_w, c_conv_b=c_conv_b,
             c_wx=c_wx, c_bx=c_bx, c_wa=c_wa, c_ba=c_ba, c_lambda=c_lambda)
    W = _prep_weights(P)
    dt = x_prompt.dtype
    bp = x_prompt.shape[0]
    n_even, n_odd = state_rwkv.shape[0], state_lru_h.shape[0]
    pb_w = state_rwkv_shift.shape[-1]
    cw = state_lru_h.shape[-1]

    mem_k, mem_v = _memory_kv(mem_prompt, norm_mem, W)
    (y_prompt, _, p_rwkv_state, p_rwkv_shift, p_lru_h, p_lru_conv, p_attn_k, p_attn_v, p_attn_kidx) = _trunk(
        x_prompt, 0,
        jnp.zeros((n_even, bp, HB, NB, NB), dt), jnp.zeros((n_even, bp, pb_w), dt),
        jnp.zeros((n_odd, bp, cw), dt), jnp.zeros((n_odd, bp, CONV_W - 1, cw), dt),
        mem_k, mem_v, P, W, None)
    mem_out = lambda ms: jnp.stack([m.reshape(m.shape[0], m.shape[1], XH, -1) for m in ms], 0)
    p_mem_k, p_mem_v = mem_out(mem_k), mem_out(mem_v)

    past_len = page_table.shape[1] * PAGE_SIZE
    (y_sample, s_chunk_v, s_rwkv_state, s_rwkv_shift, s_lru_h, s_lru_conv, s_attn_k, s_attn_v, s_attn_kidx) = _trunk(
        x_sample, past_len, state_rwkv, state_rwkv_shift, state_lru_h, state_lru_conv,
        cache_mem_k, cache_mem_v, P, W, (cache_attn_k, cache_attn_v, cache_attn_kidx, page_table))

    return (y_prompt, y_sample,
            p_rwkv_state, p_rwkv_shift, p_lru_h, p_lru_conv, p_attn_k, p_attn_v, p_attn_kidx, p_mem_k, p_mem_v,
            s_chunk_v, s_rwkv_state, s_rwkv_shift, s_lru_h, s_lru_conv, s_attn_k, s_attn_v, s_attn_kidx)
```

```python
import functools

import numpy as np
import jax
import jax.numpy as jnp
from jax import lax
from jax.experimental import pallas as pl
from jax.experimental.pallas import tpu as pltpu

F32 = jnp.float32
BF16 = jnp.bfloat16
I32 = jnp.int32
INT_MIN = -2 ** 31

EPS = 1e-6
GN_EPS = 64e-5
CHUNK = 128
A_G = 4
HB = 8
NB = 64
W_LORA = 64
A_LORA = 64
G_LORA = 128
NBLK = 8
CONV_W = 4
LRU_C = 8.0
H_D = 8
HD = 64
N_KV = 2
H_I = 8
D_I = 32
TOPK_MAX = 256
Q_BLOCK = 128
PAGE_SIZE = 128
ROPE_THETA = 500000.0
XH = 4
LANES = 128
SUBLANES = 8
VMEM_LIMIT = 56 * 1024 * 1024
NEG = -1e30
LOG2E = 1.4426950408889634
KEY_NEG_INF = -2139095041
EARLY_BITS = 28


def _params(*sem):
    return pltpu.CompilerParams(dimension_semantics=sem, vmem_limit_bytes=VMEM_LIMIT)


def _dot(a, b):
    return jnp.dot(a, b, preferred_element_type=F32)


def _dot_nt(a, b):
    return lax.dot_general(a, b, (((1,), (1,)), ((), ())), preferred_element_type=F32)


def _rms(x, g):
    return x * lax.rsqrt(jnp.mean(x * x, axis=-1, keepdims=True) + EPS) * g


def _segsum(x, ones_bf):
    hi = x.astype(BF16)
    lo = (x - hi.astype(F32)).astype(BF16)
    return _dot(hi, ones_bf) + _dot(lo, ones_bf)


def _softplus(x):
    return jnp.maximum(x, 0.0) + jnp.log1p(jnp.exp(-jnp.abs(x)))


def _full(shape):
    n = len(shape)
    return pl.BlockSpec(shape, lambda *_: (0,) * n)


def _rms_matmul_body(x_ref, g_ref, w_ref, *out_refs, splits):
    h = _rms(x_ref[...], g_ref[...]).astype(BF16)
    off = 0
    for o_ref, n in zip(out_refs, splits):
        o_ref[...] = _dot(h, w_ref[:, off:off + n])
        off += n


def rms_matmul(x, g, w_bf, splits):
    T, D = x.shape
    N = w_bf.shape[1]
    assert sum(splits) == N
    tm = min(256, T)
    return pl.pallas_call(
        functools.partial(_rms_matmul_body, splits=tuple(splits)),
        grid=(T // tm,),
        in_specs=[pl.BlockSpec((tm, D), lambda i: (i, 0)), _full((1, D)), _full((D, N))],
        out_specs=[pl.BlockSpec((tm, n), lambda i: (i, 0)) for n in splits],
        out_shape=[jax.ShapeDtypeStruct((T, n), F32) for n in splits],
        compiler_params=_params("arbitrary"), name="rms_matmul")(x, g.reshape(1, D), w_bf)


def _chunk_mixer_body(pa_ref, lng_ref, lnb_ref, m_ref, bias_ref, ya_ref, *maybe_v_ref, ng):
    z = jax.nn.gelu(pa_ref[...])
    aw = z.shape[1] // 2
    u = z[:, :aw]
    v = z[:, aw:]
    mu = jnp.mean(v, -1, keepdims=True)
    var = jnp.mean(jnp.square(v - mu), -1, keepdims=True)
    v = (v - mu) * lax.rsqrt(var + EPS) * lng_ref[...] + lnb_ref[...]
    for v_ref in maybe_v_ref:
        v_ref[...] = v
    vb = v.astype(BF16)
    ac = aw // ng
    for g in range(ng):
        s = _dot(m_ref[g], vb[:, g * ac:(g + 1) * ac]) + bias_ref[g]
        ya_ref[:, g * ac:(g + 1) * ac] = u[:, g * ac:(g + 1) * ac] * s


def chunk_mixer(pa, ln_g, ln_b, mix_bf, bias, emit_v):
    T, W = pa.shape
    aw = W // 2
    ng, tm, _ = mix_bf.shape
    nout = 2 if emit_v else 1
    return pl.pallas_call(
        functools.partial(_chunk_mixer_body, ng=ng),
        grid=(T // tm,),
        in_specs=[pl.BlockSpec((tm, W), lambda i: (i, 0)), _full((1, aw)), _full((1, aw)),
                  _full(mix_bf.shape), _full(bias.shape)],
        out_specs=[pl.BlockSpec((tm, aw), lambda i: (i, 0))] * nout,
        out_shape=[jax.ShapeDtypeStruct((T, aw), F32)] * nout,
        compiler_params=_params("arbitrary"), name="chunk_mixer")(pa, ln_g.reshape(1, aw), ln_b.reshape(1, aw), mix_bf, bias)


def _rwkv_prep_body(pb_ref, sh0_ref, mu_ref, w0_ref, wB_ref, a0_ref, aB_ref, gB_ref, ka_ref, rk_ref, ones_ref,
                    r_out, w_out, k_out, a_out, v_out, g_out, bv_out, last_out, carry_ref):
    j = pl.program_id(1)
    pb = pb_ref[0]
    tm = pb.shape[0]
    bw = r_out.shape[2]

    @pl.when(j == 0)
    def _():
        carry_ref[...] = sh0_ref[0]

    row = lax.broadcasted_iota(I32, pb.shape, 0)
    prev = jnp.where(row == 0, carry_ref[...], pltpu.roll(pb, 1, axis=0))
    last = pb[tm - 1:tm, :]
    carry_ref[...] = last
    last_out[0] = last
    pm = pb + (prev - pb) * mu_ref[...]
    r = pm[:, 0:bw]
    k = pm[:, bw:2 * bw]
    v = pm[:, 2 * bw:3 * bw]
    wa = pm[:, 3 * bw:3 * bw + W_LORA + A_LORA]
    gl = pm[:, 3 * bw + W_LORA + A_LORA:]
    ones = ones_ref[...]
    w = -_softplus(-(w0_ref[...] + _dot(jnp.tanh(wa).astype(BF16), wB_ref[...]))) - 0.5
    decay = jnp.exp(-jnp.exp(w))
    a = jax.nn.sigmoid(a0_ref[...] + _dot(wa.astype(BF16), aB_ref[...]))
    g = _dot(jax.nn.sigmoid(gl).astype(BF16), gB_ref[...])
    k2 = k * (1.0 + (a - 1.0) * ka_ref[...])
    bonus = _segsum(r * k2 * rk_ref[...], ones)
    r_out[0] = r
    w_out[0] = decay
    k_out[0] = k
    a_out[0] = a
    v_out[0] = v
    g_out[0] = g
    bv_out[0] = bonus * v


def rwkv_prep(pb3, shift0, mu, w0, wB_pad, a0, aB_pad, gB, k_a, r_k, ones_bf):
    B, S, PB = pb3.shape
    bw = w0.shape[-1]
    tm = min(256, S)
    row = lambda a: a.reshape(1, -1)
    tok = pl.BlockSpec((1, tm, bw), lambda b, j: (b, j, 0))
    outs = pl.pallas_call(
        _rwkv_prep_body,
        grid=(B, S // tm),
        in_specs=[pl.BlockSpec((1, tm, PB), lambda b, j: (b, j, 0)), pl.BlockSpec((1, 1, PB), lambda b, j: (b, 0, 0)),
                  _full((1, PB)), _full((1, bw)), _full(wB_pad.shape), _full((1, bw)), _full(aB_pad.shape),
                  _full(gB.shape), _full((1, bw)), _full((1, bw)), _full(ones_bf.shape)],
        out_specs=[tok] * 7 + [pl.BlockSpec((1, 1, PB), lambda b, j: (b, 0, 0))],
        out_shape=[jax.ShapeDtypeStruct((B, S, bw), F32)] * 7 + [jax.ShapeDtypeStruct((B, 1, PB), F32)],
        scratch_shapes=[pltpu.VMEM((1, PB), F32)],
        compiler_params=_params("arbitrary", "arbitrary"), name="rwkv_prep")(
            pb3, shift0.reshape(B, 1, PB), row(mu), row(w0), wB_pad, row(a0), aB_pad, gB, row(k_a), row(r_k), ones_bf)
    return outs


def _rwkv_scan_body(r_ref, w_ref, k_ref, a_ref, v_ref, s0_ref, kkp_ref, kap_ref, y_ref, sl_ref, S_ref, op_ref, *, tc, nip):
    c = pl.program_id(1)

    @pl.when(c == 0)
    def _():
        S_ref[...] = s0_ref[0]

    dup = lambda x: jnp.concatenate([x, x], axis=1)

    def derive(t, slot):
        k = dup(k_ref[0, t])
        a = dup(a_ref[0, t])
        kk = k * kkp_ref[...]
        kk = kk * lax.rsqrt(jnp.sum(kk * kk, axis=0, keepdims=True) + 1e-12)
        op_ref[slot, 0] = dup(r_ref[0, t])
        op_ref[slot, 1] = dup(w_ref[0, t])
        op_ref[slot, 2] = k * (1.0 + (a - 1.0) * kap_ref[...])
        op_ref[slot, 3] = kk
        op_ref[slot, 4] = kk * a

    def advance(t, slot):
        for ip in range(nip):
            Sg = S_ref[ip]
            sa = jnp.sum(Sg * op_ref[slot, 3], axis=0, keepdims=True)
            vrow = v_ref[0, t, pl.ds(ip, 1), :]
            Sg = Sg * op_ref[slot, 1] - sa * op_ref[slot, 4] + vrow * op_ref[slot, 2]
            S_ref[ip] = Sg
            y_ref[0, t, pl.ds(ip, 1), :] = jnp.sum(Sg * op_ref[slot, 0], axis=0, keepdims=True)

    derive(0, 0)

    def step_pair(u, carry):
        t0 = 2 * u
        derive(t0 + 1, 1)
        advance(t0, 0)
        derive(jnp.minimum(t0 + 2, tc - 1), 0)
        advance(t0 + 1, 1)
        return carry

    assert tc % 2 == 0
    lax.fori_loop(0, tc // 2, step_pair, 0)

    @pl.when(c == pl.num_programs(1) - 1)
    def _():
        sl_ref[0] = S_ref[...]


def rwkv_scan(r, w, k, a, v, s0, kkp, kap):
    G, S, nb, hl = r.shape
    nip, L = v.shape[2:]
    tc = min(64, S)
    vec = pl.BlockSpec((1, tc, nb, hl), lambda g, c: (g, c, 0, 0))
    vsp = pl.BlockSpec((1, tc, nip, L), lambda g, c: (g, c, 0, 0))
    ssp = pl.BlockSpec((1, nip, nb, L), lambda g, c: (g, 0, 0, 0))
    return pl.pallas_call(
        functools.partial(_rwkv_scan_body, tc=tc, nip=nip),
        grid=(G, S // tc),
        in_specs=[vec] * 4 + [vsp, ssp, _full((nb, L)), _full((nb, L))],
        out_specs=[vsp, ssp],
        out_shape=[jax.ShapeDtypeStruct((G, S, nip, L), F32), jax.ShapeDtypeStruct((G, nip, nb, L), F32)],
        scratch_shapes=[pltpu.VMEM((nip, nb, L), F32), pltpu.VMEM((2, 5, nb, L), F32)],
        compiler_params=_params("arbitrary", "arbitrary"), name="rwkv_scan")(r, w, k, a, v, s0, kkp, kap)


def _to_scan(x):
    G, Bg, S, _ = x.shape
    return x.reshape(G, Bg, S, HB, NB).transpose(0, 2, 4, 1, 3).reshape(G, S, NB, Bg * HB)


def _param_to_scan(p, Bg):
    return jnp.tile(p.reshape(HB, NB).T, (1, 2 * Bg))


def _v_to_scan(x):
    G, Bg, S, _ = x.shape
    return x.reshape(G, Bg, S, HB, 2, NB // 2).transpose(0, 2, 5, 4, 1, 3).reshape(G, S, NB // 2, 2 * Bg * HB)


def _y_from_scan(y, Bg):
    G, S = y.shape[:2]
    return y.reshape(G, S, NB // 2, 2, Bg, HB).transpose(0, 4, 1, 5, 3, 2).reshape(G, Bg, S, HB * NB)


def _state_to_scan(s):
    G, Bg = s.shape[:2]
    return s.reshape(G, Bg, HB, 2, NB // 2, NB).transpose(0, 4, 5, 3, 1, 2).reshape(G, NB // 2, NB, 2 * Bg * HB)


def _state_from_scan(s, Bg):
    G = s.shape[0]
    return s.reshape(G, NB // 2, NB, 2, Bg, HB).transpose(0, 4, 5, 3, 1, 2).reshape(G, Bg, HB, NB, NB)


def _mix0_out_body(ya_ref, y_ref, bv_ref, g_ref, x_ref, gng_ref, gnb_ref, ones_ref, w_ref, o_ref):
    ones = ones_ref[...]
    y = y_ref[...]
    mean = _segsum(y, ones) * (1.0 / NB)
    d = y - mean
    var = _segsum(d * d, ones) * (1.0 / NB)
    yb = (d * lax.rsqrt(var + GN_EPS) * gng_ref[...] + gnb_ref[...] + bv_ref[...]) * g_ref[...]
    aw = ya_ref.shape[1]
    o_ref[...] = x_ref[...] + _dot(ya_ref[...].astype(BF16), w_ref[:aw, :]) + _dot(yb.astype(BF16), w_ref[aw:, :])


def mix0_out(ya, y, bv, g, x, gn_g, gn_b, ones_bf, w_bf):
    T, D = x.shape
    aw = ya.shape[1]
    bw = y.shape[1]
    tm = min(256, T)
    tok = lambda n: pl.BlockSpec((tm, n), lambda i: (i, 0))
    return pl.pallas_call(
        _mix0_out_body,
        grid=(T // tm,),
        in_specs=[tok(aw), tok(bw), tok(bw), tok(bw), tok(D), _full((1, bw)), _full((1, bw)), _full(ones_bf.shape),
                  _full(w_bf.shape)],
        out_specs=tok(D),
        out_shape=jax.ShapeDtypeStruct((T, D), F32),
        compiler_params=_params("arbitrary"), name="mix0_out")(ya, y, bv, g, x, gn_g.reshape(1, bw), gn_b.reshape(1, bw),
                                                              ones_bf, w_bf)


def _mix1_out_body(a1_ref, a2_ref, x_ref, w_ref, o_ref):
    n1 = a1_ref.shape[1]
    o_ref[...] = x_ref[...] + _dot(a1_ref[...].astype(BF16), w_ref[:n1, :]) + _dot(a2_ref[...].astype(BF16), w_ref[n1:, :])


def mix1_out(a1, a2, x, w_bf):
    T, D = x.shape
    tm = min(256, T)
    tok = lambda n: pl.BlockSpec((tm, n), lambda i: (i, 0))
    return pl.pallas_call(
        _mix1_out_body,
        grid=(T // tm,),
        in_specs=[tok(a1.shape[1]), tok(a2.shape[1]), tok(D), _full(w_bf.shape)],
        out_specs=tok(D),
        out_shape=jax.ShapeDtypeStruct((T, D), F32),
        compiler_params=_params("arbitrary"), name="mix1_out")(a1, a2, x, w_bf)


def _xattn_body(x_ref, g_ref, wq_ref, mk_ref, mv_ref, wo_ref, o_ref, *, nh):
    x = x_ref[0]
    h = _rms(x, g_ref[...]).astype(BF16)
    q = _dot(h, wq_ref[...])
    mk = mk_ref[0].astype(BF16)
    mv = mv_ref[0].astype(BF16)
    hd = q.shape[1] // nh
    outs = []
    for hh in range(nh):
        sl = slice(hh * hd, (hh + 1) * hd)
        lg = _dot_nt(q[:, sl].astype(BF16), mk[:, sl]) * (hd ** -0.5)
        e = jnp.exp(lg - jnp.max(lg, -1, keepdims=True))
        p = e / jnp.sum(e, -1, keepdims=True)
        outs.append(_dot(p.astype(BF16), mv[:, sl]))
    o = jnp.concatenate(outs, axis=1).astype(BF16)
    o_ref[0] = x + _dot(o, wo_ref[...])


def cross_attn(x3, g, wq_bf, mk, mv, wo_bf):
    B, S, D = x3.shape
    M = mk.shape[1]
    tm = min(256, S)
    return pl.pallas_call(
        functools.partial(_xattn_body, nh=XH),
        grid=(B, S // tm),
        in_specs=[pl.BlockSpec((1, tm, D), lambda b, j: (b, j, 0)), _full((1, D)), _full(wq_bf.shape),
                  pl.BlockSpec((1, M, D), lambda b, j: (b, 0, 0)), pl.BlockSpec((1, M, D), lambda b, j: (b, 0, 0)),
                  _full(wo_bf.shape)],
        out_specs=pl.BlockSpec((1, tm, D), lambda b, j: (b, j, 0)),
        out_shape=jax.ShapeDtypeStruct((B, S, D), F32),
        compiler_params=_params("arbitrary", "arbitrary"), name="cross_attn")(x3, g.reshape(1, D), wq_bf, mk, mv, wo_bf)


def _ffn_body(x_ref, g_ref, wg_ref, wu_ref, wd_ref, gf_ref, o_ref, *, final_norm):
    x = x_ref[...]
    h = _rms(x, g_ref[...]).astype(BF16)
    gate = _dot(h, wg_ref[...])
    up = _dot(h, wu_ref[...])
    act = (jax.nn.silu(gate) * up).astype(BF16)
    y = x + _dot(act, wd_ref[...])
    if final_norm:
        y = _rms(y, gf_ref[...])
    o_ref[...] = y


def ffn(x, g, wg_bf, wu_bf, wd_bf, g_final, final_norm):
    T, D = x.shape
    tm = min(256, T)
    return pl.pallas_call(
        functools.partial(_ffn_body, final_norm=final_norm),
        grid=(T // tm,),
        in_specs=[pl.BlockSpec((tm, D), lambda i: (i, 0)), _full((1, D)), _full(wg_bf.shape), _full(wu_bf.shape),
                  _full(wd_bf.shape), _full((1, D))],
        out_specs=pl.BlockSpec((tm, D), lambda i: (i, 0)),
        out_shape=jax.ShapeDtypeStruct((T, D), F32),
        compiler_params=_params("arbitrary"), name="ffn")(x, g.reshape(1, D), wg_bf, wu_bf, wd_bf, g_final.reshape(1, D))


def _shift_rows(x, d, fill):
    row = lax.broadcasted_iota(I32, x.shape, 0)
    return jnp.where(row >= d, pltpu.roll(x, d, axis=0), fill)


def _lru_body(pc_ref, h0_ref, cv0_ref, cw_ref, cb_ref, wx_ref, bx_ref, wa_ref, ba_ref, lam_ref,
              y_out, h_out, cv_out, hc_ref, cc_ref, *, reset_first):
    j = pl.program_id(1)
    pc = pc_ref[0]
    tm = pc.shape[0]
    cw = pc.shape[1] // 2
    gate_pre = pc[:, :cw]
    x = pc[:, cw:]

    @pl.when(j == 0)
    def _():
        hc_ref[...] = h0_ref[0]
        cc_ref[...] = cv0_ref[0]

    carry = cc_ref[...]
    row8 = lax.broadcasted_iota(I32, (SUBLANES, cw), 0)
    y = cb_ref[...] + x * cw_ref[CONV_W - 1:CONV_W, :]
    for d in range(1, CONV_W):
        xs = pltpu.roll(x, d, axis=0)
        first = jnp.where(row8 < d, pltpu.roll(carry, d, axis=0), xs[:SUBLANES])
        xd = first if tm == SUBLANES else jnp.concatenate([first, xs[SUBLANES:]], axis=0)
        y = y + xd * cw_ref[CONV_W - 1 - d:CONV_W - d, :]
    new_carry = x[tm - SUBLANES:, :]
    cc_ref[...] = new_carry
    cv_out[0] = new_carry

    yb = y.astype(BF16)
    gx = jax.nn.sigmoid(_dot(yb, wx_ref[...]) + bx_ref[...])
    ga = jax.nn.sigmoid(_dot(yb, wa_ref[...]) + ba_ref[...])
    log_a = LRU_C * ga * (-_softplus(-lam_ref[...]))
    a = jnp.exp(log_a)
    th = jnp.tanh(log_a)
    mult = jnp.sqrt(-2.0 * th / (1.0 - th))
    if reset_first:
        row = lax.broadcasted_iota(I32, (tm, cw), 0)
        mult = jnp.where((row == 0) & (j == 0), 1.0, mult)
    bt = y * gx * mult
    d = 1
    while d < tm:
        bt = a * _shift_rows(bt, d, 0.0) + bt
        a = a * _shift_rows(a, d, 1.0)
        d *= 2
    h = bt + a * hc_ref[...]
    h_last = h[tm - 1:tm, :]
    hc_ref[...] = h_last
    h_out[0] = h_last
    y_out[0] = h * jax.nn.gelu(gate_pre)


def lru_mixer(pc3, h0, conv0, conv_w, conv_b, wx_bd, bx, wa_bd, ba, lam, reset_first):
    B, S, W = pc3.shape
    cw = W // 2
    tm = min(256, S)
    cv0 = jnp.concatenate([jnp.zeros((B, SUBLANES - (CONV_W - 1), cw), F32), conv0], axis=1)
    row = lambda a: a.reshape(1, cw)
    y, h_last, cv = pl.pallas_call(
        functools.partial(_lru_body, reset_first=reset_first),
        grid=(B, S // tm),
        in_specs=[pl.BlockSpec((1, tm, W), lambda b, j: (b, j, 0)), pl.BlockSpec((1, 1, cw), lambda b, j: (b, 0, 0)),
                  pl.BlockSpec((1, SUBLANES, cw), lambda b, j: (b, 0, 0)), _full((CONV_W, cw)), _full((1, cw)),
                  _full(wx_bd.shape), _full((1, cw)), _full(wa_bd.shape), _full((1, cw)), _full((1, cw))],
        out_specs=[pl.BlockSpec((1, tm, cw), lambda b, j: (b, j, 0)), pl.BlockSpec((1, 1, cw), lambda b, j: (b, 0, 0)),
                   pl.BlockSpec((1, SUBLANES, cw), lambda b, j: (b, 0, 0))],
        out_shape=[jax.ShapeDtypeStruct((B, S, cw), F32), jax.ShapeDtypeStruct((B, 1, cw), F32),
                   jax.ShapeDtypeStruct((B, SUBLANES, cw), F32)],
        scratch_shapes=[pltpu.VMEM((1, cw), F32), pltpu.VMEM((SUBLANES, cw), F32)],
        compiler_params=_params("arbitrary", "arbitrary"), name="lru_mixer")(
            pc3, h0.reshape(B, 1, cw), cv0, conv_w, row(conv_b), wx_bd, row(bx), wa_bd, row(ba), row(lam))
    return y, h_last[:, 0], cv[:, SUBLANES - (CONV_W - 1):]


def _rope_tables(pos, head_dim):
    rot = head_dim // 4
    half = rot // 2
    inv = ROPE_THETA ** (-jnp.arange(half, dtype=F32) * 2.0 / rot)
    ang = pos.astype(F32)[:, None] * inv[None, :]
    cos, sin = jnp.cos(ang), jnp.sin(ang)
    S = pos.shape[0]
    rest = head_dim - rot
    c = jnp.concatenate([cos, cos, jnp.ones((S, rest), F32)], -1)
    sa = jnp.concatenate([-sin, jnp.zeros((S, head_dim - half), F32)], -1)
    sb = jnp.concatenate([jnp.zeros((S, half), F32), sin, jnp.zeros((S, rest), F32)], -1)
    rep = LANES // head_dim
    return tuple(jnp.tile(t, (1, rep)) for t in (c, sa, sb))


def _rope(x, c, sa, sb, half):
    n = x.shape[1] // LANES
    tile = (lambda t: t) if n == 1 else (lambda t: jnp.concatenate([t] * n, axis=1))
    w = x.shape[1]
    return x * tile(c) + pltpu.roll(x, w - half, axis=1) * tile(sa) + pltpu.roll(x, half, axis=1) * tile(sb)


def _rope_all(q_ref, k_ref, qi_ref, kiw_ref, cq_ref, saq_ref, sbq_ref, ci_ref, sai_ref, sbi_ref):
    qr = _rope(q_ref[...], cq_ref[...], saq_ref[...], sbq_ref[...], HD // 8)
    kr = _rope(k_ref[...], cq_ref[...], saq_ref[...], sbq_ref[...], HD // 8)
    qir = _rope(qi_ref[...], ci_ref[...], sai_ref[...], sbi_ref[...], D_I // 8)
    kiw = kiw_ref[...]
    lane = lax.broadcasted_iota(I32, kiw.shape, 1)
    iski = lane < D_I
    kiwr = _rope(kiw, jnp.where(iski, ci_ref[...], 1.0), jnp.where(iski, sai_ref[...], 0.0),
                 jnp.where(iski, sbi_ref[...], 0.0), D_I // 8)
    return qr, kr, qir, kiwr


def _dsa_prep_sample_body(q_ref, k_ref, qi_ref, kiw_ref, cq_ref, saq_ref, sbq_ref, ci_ref, sai_ref, sbi_ref,
                          qr_out, kr_out, qir_out, kiwr_out):
    qr, kr, qir, kiwr = _rope_all(q_ref, k_ref, qi_ref, kiw_ref, cq_ref, saq_ref, sbq_ref, ci_ref, sai_ref, sbi_ref)
    qr_out[...] = qr
    kr_out[...] = kr
    qir_out[...] = qir
    kiwr_out[...] = kiwr


def _dsa_prep_prompt_body(q_ref, k_ref, v_ref, qi_ref, kiw_ref, cq_ref, saq_ref, sbq_ref, ci_ref, sai_ref, sbi_ref,
                          kr_out, kiwr_out, wq_out, wi_out, wrow_out, kbf_out, kib_out, vt_out):
    qr, kr, qir, kiwr = _rope_all(q_ref, k_ref, qi_ref, kiw_ref, cq_ref, saq_ref, sbq_ref, ci_ref, sai_ref, sbi_ref)
    tm = qr.shape[0]
    kr_out[...] = kr
    kiwr_out[...] = kiwr
    kbf_out[...] = kr.astype(BF16)
    kib_out[...] = kiwr.astype(BF16)
    vt_out[0] = v_ref[...].T.astype(BF16)
    qt = (qr * (HD ** -0.5 * LOG2E)).T
    z = jnp.zeros((HD, tm), F32)
    slabs = []
    for h in range(H_D):
        piece = qt[h * HD:(h + 1) * HD, :]
        slabs.append(jnp.concatenate([piece, z] if h < H_D // N_KV else [z, piece], axis=0))
    wq_out[0] = jnp.concatenate(slabs, axis=1).astype(BF16)
    qit = qir.T
    zi = jnp.zeros((LANES - D_I, tm), F32)
    wi_out[0] = jnp.concatenate([jnp.concatenate([qit[h * D_I:(h + 1) * D_I, :], zi], axis=0) for h in range(H_I)],
                                axis=1).astype(BF16)
    wrow_out[0] = kiwr.T[D_I:D_I + H_I, :] * (H_I ** -0.5 * D_I ** -0.5)


def dsa_prep(q, k, v, qi, kiw, tabs_q, tabs_i, prompt):
    T = q.shape[0]
    tm = Q_BLOCK if prompt else min(256, T)
    tok = lambda n: pl.BlockSpec((tm, n), lambda i: (i, 0))
    nq, nk, ni = q.shape[1], k.shape[1], qi.shape[1]
    tabs = list(tabs_q) + list(tabs_i)
    if not prompt:
        return pl.pallas_call(
            _dsa_prep_sample_body,
            grid=(T // tm,),
            in_specs=[tok(nq), tok(nk), tok(ni), tok(LANES)] + [tok(LANES)] * 6,
            out_specs=[tok(nq), tok(nk), tok(ni), tok(LANES)],
            out_shape=[jax.ShapeDtypeStruct((T, n), F32) for n in (nq, nk, ni, LANES)],
            compiler_params=_params("arbitrary"), name="dsa_prep_sample")(q, k, qi, kiw, *tabs)
    nblk = T // tm
    blk = lambda r, c: pl.BlockSpec((1, r, c), lambda i: (i, 0, 0))
    nper = tabs[0].shape[0] // tm
    return pl.pallas_call(
        _dsa_prep_prompt_body,
        grid=(nblk,),
        in_specs=[tok(nq), tok(nk), tok(nk), tok(ni), tok(LANES)] + [pl.BlockSpec((tm, LANES), lambda i: (i % nper, 0))] * 6,
        out_specs=[tok(nk), tok(LANES), blk(nk, H_D * tm), blk(LANES, H_I * tm), blk(H_I, tm), tok(nk), tok(LANES),
                   blk(nk, tm)],
        out_shape=[jax.ShapeDtypeStruct((T, nk), F32), jax.ShapeDtypeStruct((T, LANES), F32),
                   jax.ShapeDtypeStruct((nblk, nk, H_D * tm), BF16), jax.ShapeDtypeStruct((nblk, LANES, H_I * tm), BF16),
                   jax.ShapeDtypeStruct((nblk, H_I, tm), F32), jax.ShapeDtypeStruct((T, nk), BF16),
                   jax.ShapeDtypeStruct((T, LANES), BF16), jax.ShapeDtypeStruct((nblk, nk, tm), BF16)],
        compiler_params=_params("arbitrary"), name="dsa_prep_prompt")(q, k, v, qi, kiw, *tabs)


def _key_to_f32(k):
    return pltpu.bitcast(k ^ ((k >> 31) & 0x7FFFFFFF), F32)


def _kth_largest(count_ge, shape, k):
    def step(it, key):
        cand = key + jnp.left_shift(jnp.int32(1), 31 - it)
        return jnp.where(count_ge(_key_to_f32(cand)) >= k, cand, key)

    decode = lambda key: jnp.where(key <= KEY_NEG_INF, -jnp.inf, _key_to_f32(key))
    key = lax.fori_loop(0, EARLY_BITS, step, jnp.full(shape, INT_MIN, I32))
    unresolved = (count_ge(decode(key)) != k) & (key > KEY_NEG_INF)
    more = jnp.where(jnp.max(jnp.where(unresolved, 1, 0)) > 0, 32 - EARLY_BITS, 0)
    key = lax.fori_loop(EARLY_BITS, EARLY_BITS + more, step, key)
    return decode(key)


def _dsa_prompt_body(wq_ref, wi_ref, wrow_ref, kbf_ref, kib_ref, vt_ref, o_ref,
                     sc_ref, acc_ref, bias_ref, *, topk, kbs):
    i = pl.program_id(1)
    qb = Q_BLOCK
    q0 = i * qb
    ls = kbs * qb
    la = ls // 2
    ns = (q0 + qb + ls - 1) // ls
    sub = lax.broadcasted_iota(I32, (qb, qb), 0)
    lane = lax.broadcasted_iota(I32, (qb, qb), 1)
    wi = wi_ref[0]
    wrow = wrow_ref[0]

    def score_block(kb, carry):
        k0 = kb * ls
        r = _dot(kib_ref[0, pl.ds(k0, ls), :], wi)
        for c in range(kbs):
            rc = r[c * qb:(c + 1) * qb, :]
            acc = wrow[0:1, :] * jnp.maximum(rc[:, 0:qb], 0.0)
            for h in range(1, H_I):
                acc = acc + wrow[h:h + 1, :] * jnp.maximum(rc[:, h * qb:(h + 1) * qb], 0.0)
            causal = (k0 + c * qb + sub) <= (q0 + lane)
            sc_ref[pl.ds(k0 + c * qb, qb), :] = jnp.where(causal, acc, -jnp.inf)
        return carry

    lax.fori_loop(0, ns, score_block, 0)

    def count(pred):
        def body(kb, cnt):
            for c in range(kbs):
                m = jnp.where(pred(sc_ref[pl.ds(kb * ls + c * qb, qb), :]), 1, 0)
                cnt = cnt + jnp.sum(m.reshape(qb // SUBLANES, SUBLANES, qb), axis=0)
            return cnt
        cnt = lax.fori_loop(0, ns, body, jnp.zeros((SUBLANES, qb), I32))
        return jnp.sum(cnt, axis=0, keepdims=True)

    thr = _kth_largest(lambda t: count(lambda s: s >= t), (1, qb), topk)
    need = (topk - count(lambda s: s > thr)).astype(F32)
    excess = (count(lambda s: s >= thr) > topk) & (thr > -jnp.inf)

    @pl.when(jnp.max(jnp.where(excess, 1, 0)) > 0)
    def _():
        ltri = jnp.where(lane < sub, 1.0, 0.0).astype(BF16)

        def body(c, run):
            s = sc_ref[pl.ds(c * qb, qb), :]
            eq = s == thr
            eqf = jnp.where(eq, 1.0, 0.0)
            pre = _dot(ltri, eqf.astype(BF16)) + run
            sc_ref[pl.ds(c * qb, qb), :] = jnp.where(eq & (pre >= need), -jnp.inf, s)
            return run + jnp.sum(eqf, axis=0, keepdims=True)

        lax.fori_loop(0, ns * kbs, body, jnp.zeros((1, qb), F32))

    acc_ref[...] = jnp.zeros(acc_ref.shape, F32)
    hpg = H_D // N_KV

    def attend(kb, carry):
        ms, ls_ = carry
        logits = _dot(kbf_ref[0, pl.ds(kb * la, la), :], wq_ref[0])
        s = sc_ref[pl.ds(kb * la, la), :]
        bias_ref[...] = jnp.where((s >= thr) & (s > -jnp.inf), 0.0, NEG)
        vt = vt_ref[0, kb]
        ms_new, ls_new = [], []
        for g in range(N_KV):
            ps, alphas = [], []
            for h in range(g * hpg, (g + 1) * hpg):
                lg = logits[:, h * qb:(h + 1) * qb] + bias_ref[...]
                m_new = jnp.maximum(ms[h], jnp.max(lg, axis=0, keepdims=True))
                p = jnp.exp2(lg - m_new)
                alpha = jnp.exp2(ms[h] - m_new)
                ms_new.append(m_new)
                ls_new.append(alpha * ls_[h] + jnp.sum(p, axis=0, keepdims=True))
                ps.append(p.astype(BF16))
                alphas.append(alpha)
            gs = slice(g * hpg * qb, (g + 1) * hpg * qb)
            acc_ref[:, gs] = (acc_ref[:, gs] * jnp.concatenate(alphas, axis=1)
                              + _dot(vt[g * HD:(g + 1) * HD, :], jnp.concatenate(ps, axis=1)))
        return tuple(ms_new), tuple(ls_new)

    init = (tuple(jnp.full((1, qb), NEG, F32) for _ in range(H_D)), tuple(jnp.zeros((1, qb), F32) for _ in range(H_D)))
    _, l_fin = lax.fori_loop(0, (q0 + qb + la - 1) // la, attend, init)
    o_ref[0] = jnp.concatenate([acc_ref[:, h * qb:(h + 1) * qb] * (1.0 / l_fin[h]) for h in range(H_D)], axis=0).T


def dsa_prompt(wq, wi, wrow, kbf, kib, vt, B, S):
    nq = S // Q_BLOCK
    topk = min(TOPK_MAX, S // 4)
    assert nq % 2 == 0
    kbs = 4 if nq % 4 == 0 else 2
    kba = kbs // 2
    la = kba * Q_BLOCK
    nk = kbf.shape[-1]
    vt4 = vt.reshape(B, S // la, kba, nk, Q_BLOCK).transpose(0, 1, 3, 2, 4).reshape(B, S // la, nk, la)
    qspec = lambda r, c: pl.BlockSpec((1, r, c), lambda b, i: (b * nq + i, 0, 0))
    return pl.pallas_call(
        functools.partial(_dsa_prompt_body, topk=topk, kbs=kbs),
        grid=(B, nq),
        in_specs=[qspec(nk, H_D * Q_BLOCK), qspec(LANES, H_I * Q_BLOCK), qspec(H_I, Q_BLOCK),
                  pl.BlockSpec((1, S, nk), lambda b, i: (b, 0, 0)), pl.BlockSpec((1, S, LANES), lambda b, i: (b, 0, 0)),
                  pl.BlockSpec((1, S // la, nk, la), lambda b, i: (b, 0, 0, 0))],
        out_specs=pl.BlockSpec((1, Q_BLOCK, H_D * HD), lambda b, i: (b, i, 0)),
        out_shape=jax.ShapeDtypeStruct((B, S, H_D * HD), F32),
        scratch_shapes=[pltpu.VMEM((S, Q_BLOCK), F32), pltpu.VMEM((HD, H_D * Q_BLOCK), F32),
                        pltpu.VMEM((la, Q_BLOCK), F32)],
        compiler_params=_params("arbitrary", "arbitrary"), name="dsa_prompt")(
            wq, wi, wrow, kbf.reshape(B, S, nk), kib.reshape(B, S, LANES), vt4)


def _page_copies(pt_ref, pool_ref, buf_ref, sem_ref, step, slot, pps):
    return [pltpu.make_async_copy(pool_ref.at[pt_ref[step * pps + i]], buf_ref.at[slot, i], sem_ref.at[slot])
            for i in range(pps)]


def _index_scores(qi, wb, pages_bf, t):
    r = _dot(qi, pages_bf)
    acc = wb[0:t, :] * jnp.maximum(r[0:t, :], 0.0)
    for h in range(1, H_I):
        acc = acc + wb[h * t:(h + 1) * t, :] * jnp.maximum(r[h * t:(h + 1) * t, :], 0.0)
    return acc


def _dsa_s_scores_body(pt_ref, qi_ref, wb_ref, new_ref, pool_ref, s_ref, snew_ref, buf_ref, sem_ref, *, pps, nc, t):
    s = pl.program_id(0)
    slot = s % 2
    copies = functools.partial(_page_copies, pt_ref, pool_ref, buf_ref, sem_ref, pps=pps)

    @pl.when(s == 0)
    def _():
        for cp in copies(s, slot):
            cp.start()

    @pl.when(s + 1 < pl.num_programs(0))
    def _():
        for cp in copies(s + 1, 1 - slot):
            cp.start()

    qi = qi_ref[0]
    wb = wb_ref[0]

    @pl.when(s % nc == 0)
    def _():
        acc = _index_scores(qi, wb, new_ref[0].astype(BF16), t)
        sub = lax.broadcasted_iota(I32, acc.shape, 0)
        lane = lax.broadcasted_iota(I32, acc.shape, 1)
        snew_ref[0] = jnp.where(lane <= sub, acc, -jnp.inf)

    for cp in copies(s, slot):
        cp.wait()
    pages = buf_ref[slot]
    for i in range(pps):
        s_ref[0, i] = _index_scores(qi, wb, pages[i].astype(BF16), t)


def _dsa_s_attend_body(pt_ref, qm_ref, sp_ref, sn_ref, knew_ref, vnew_ref, kpool_ref, vpool_ref, o_ref,
                       kbuf_ref, vbuf_ref, sem_ref, sc_ref, thr_ref, m_ref, l_ref, acc_ref, *, pps, nc, t, topk):
    s = pl.program_id(0)
    c = s % nc
    slot = s % 2
    npg = nc * pps
    kcopies = functools.partial(_page_copies, pt_ref, kpool_ref, kbuf_ref, sem_ref.at[0], pps=pps)
    vcopies = functools.partial(_page_copies, pt_ref, vpool_ref, vbuf_ref, sem_ref.at[1], pps=pps)

    @pl.when(s == 0)
    def _():
        for cp in kcopies(s, slot) + vcopies(s, slot):
            cp.start()

    @pl.when(s + 1 < pl.num_programs(0))
    def _():
        for cp in kcopies(s + 1, 1 - slot) + vcopies(s + 1, 1 - slot):
            cp.start()

    @pl.when(c == 0)
    def _():
        sc_ref[0:npg] = sp_ref[0]
        sc_ref[npg] = sn_ref[0]

        def count(pred):
            cnt = jnp.zeros((t, PAGE_SIZE), I32)
            for ch in range(npg + 1):
                cnt = cnt + jnp.where(pred(sc_ref[ch]), 1, 0)
            return jnp.sum(cnt, axis=1, keepdims=True)

        thr = _kth_largest(lambda x: count(lambda sc: sc >= x), (t, 1), topk)
        need = (topk - count(lambda sc: sc > thr)).astype(F32)
        excess = (count(lambda sc: sc >= thr) > topk) & (thr > -jnp.inf)

        @pl.when(jnp.max(jnp.where(excess, 1, 0)) > 0)
        def _():
            sub = lax.broadcasted_iota(I32, (PAGE_SIZE, PAGE_SIZE), 0)
            lane = lax.broadcasted_iota(I32, (PAGE_SIZE, PAGE_SIZE), 1)
            utri = jnp.where(sub < lane, 1.0, 0.0).astype(BF16)

            def body(ch, run):
                sc = sc_ref[ch]
                eq = sc == thr
                eqf = jnp.where(eq, 1.0, 0.0)
                pre = _dot(eqf.astype(BF16), utri) + run
                sc_ref[ch] = jnp.where(eq & (pre >= need), -jnp.inf, sc)
                return run + jnp.sum(eqf, axis=1, keepdims=True)

            lax.fori_loop(0, npg + 1, body, jnp.zeros((t, 1), F32))

        thr_ref[...] = jnp.broadcast_to(thr, thr_ref.shape)
        m_ref[...] = jnp.full(m_ref.shape, NEG, F32)
        l_ref[...] = jnp.zeros(l_ref.shape, F32)
        acc_ref[...] = jnp.zeros(acc_ref.shape, F32)

    qm = qm_ref[0]
    thr = thr_ref[...]

    def bias_rows(sc):
        b = jnp.where((sc >= thr) & (sc > -jnp.inf), 0.0, NEG)
        return jnp.concatenate([b] * H_D, axis=0)

    def update(k_bf, v_bf, bias):
        lg = _dot(qm, k_bf) + bias
        m_old = m_ref[...]
        m_new = jnp.maximum(m_old, jnp.max(lg, axis=1, keepdims=True))
        p = jnp.exp2(lg - m_new)
        alpha = jnp.exp2(m_old - m_new)
        m_ref[...] = m_new
        l_ref[...] = alpha * l_ref[...] + jnp.sum(p, axis=1, keepdims=True)
        acc_ref[...] = alpha * acc_ref[...] + _dot_nt(p.astype(BF16), v_bf)

    for cp in kcopies(s, slot) + vcopies(s, slot):
        cp.wait()
    bias = jnp.concatenate([bias_rows(sc_ref[c * pps + i]) for i in range(pps)], axis=1)
    pages = lambda buf: jnp.concatenate([buf[slot, i].astype(BF16) for i in range(pps)], axis=1)
    update(pages(kbuf_ref), pages(vbuf_ref), bias)

    @pl.when(c == nc - 1)
    def _():
        update(knew_ref[0].astype(BF16), vnew_ref[0].astype(BF16), bias_rows(sc_ref[npg]))
        o_ref[0] = acc_ref[...] / l_ref[...]


def dsa_sample(qr, kr, v, qir, kiwr, pool_k, pool_v, pool_ki, page_table, B, T):
    npg = page_table.shape[1]
    npool = pool_k.shape[0]
    nkv = N_KV * HD
    topk = min(TOPK_MAX, (npg * PAGE_SIZE + T) // 4)
    pps = 64 if npg % 64 == 0 else npg
    nc = npg // pps
    page_t = lambda pool: jnp.swapaxes(pool.reshape(npool, PAGE_SIZE, -1), 1, 2)
    pad_new = lambda a: jnp.swapaxes(jnp.pad(a.reshape(B, T, -1), ((0, 0), (0, PAGE_SIZE - T), (0, 0))), 1, 2)
    ki_new = pad_new(kiwr[:, :D_I])
    k_new = pad_new(kr)
    v_new = pad_new(v)
    qi_m = qir.reshape(B, T, H_I, D_I).transpose(0, 2, 1, 3).reshape(B, H_I * T, D_I).astype(BF16)
    wi = kiwr[:, D_I:D_I + H_I].reshape(B, T, H_I) * (H_I ** -0.5 * D_I ** -0.5)
    wb = jnp.broadcast_to(wi.transpose(0, 2, 1).reshape(B, H_I * T, 1), (B, H_I * T, PAGE_SIZE))
    pt_flat = page_table.reshape(-1)
    per_b = lambda r, c: pl.BlockSpec((1, r, c), lambda s, pt: (s // nc, 0, 0))
    hbm = pl.BlockSpec(memory_space=pl.ANY)
    sc_past, sc_new = pl.pallas_call(
        functools.partial(_dsa_s_scores_body, pps=pps, nc=nc, t=T),
        grid_spec=pltpu.PrefetchScalarGridSpec(
            num_scalar_prefetch=1, grid=(B * nc,),
            in_specs=[per_b(H_I * T, D_I), per_b(H_I * T, PAGE_SIZE), per_b(D_I, PAGE_SIZE), hbm],
            out_specs=[pl.BlockSpec((1, pps, T, PAGE_SIZE), lambda s, pt: (s // nc, s % nc, 0, 0)), per_b(T, PAGE_SIZE)],
            scratch_shapes=[pltpu.VMEM((2, pps, D_I, PAGE_SIZE), F32), pltpu.SemaphoreType.DMA((2,))]),
        out_shape=[jax.ShapeDtypeStruct((B, npg, T, PAGE_SIZE), F32), jax.ShapeDtypeStruct((B, T, PAGE_SIZE), F32)],
        compiler_params=_params("arbitrary"), name="dsa_sample_scores")(pt_flat, qi_m, wb, ki_new, page_t(pool_ki))
    q4 = (qr * (HD ** -0.5 * LOG2E)).reshape(B, T, N_KV, H_D // N_KV, HD).transpose(0, 2, 3, 1, 4)
    eye = jnp.eye(N_KV, dtype=F32)
    qm = (q4[:, :, :, :, None, :] * eye[None, :, None, None, :, None]).reshape(B, H_D * T, nkv).astype(BF16)
    out = pl.pallas_call(
        functools.partial(_dsa_s_attend_body, pps=pps, nc=nc, t=T, topk=topk),
        grid_spec=pltpu.PrefetchScalarGridSpec(
            num_scalar_prefetch=1, grid=(B * nc,),
            in_specs=[per_b(H_D * T, nkv), pl.BlockSpec((1, npg, T, PAGE_SIZE), lambda s, pt: (s // nc, 0, 0, 0)),
                      per_b(T, PAGE_SIZE), per_b(nkv, PAGE_SIZE), per_b(nkv, PAGE_SIZE), hbm, hbm],
            out_specs=per_b(H_D * T, nkv),
            scratch_shapes=[pltpu.VMEM((2, pps, nkv, PAGE_SIZE), F32), pltpu.VMEM((2, pps, nkv, PAGE_SIZE), F32),
                            pltpu.SemaphoreType.DMA((2, 2)), pltpu.VMEM((npg + 1, T, PAGE_SIZE), F32),
                            pltpu.VMEM((T, PAGE_SIZE), F32), pltpu.VMEM((H_D * T, 1), F32),
                            pltpu.VMEM((H_D * T, 1), F32), pltpu.VMEM((H_D * T, nkv), F32)]),
        out_shape=jax.ShapeDtypeStruct((B, H_D * T, nkv), F32),
        compiler_params=_params("arbitrary"), name="dsa_sample_attend")(
            pt_flat, qm, sc_past, sc_new, k_new, v_new, page_t(pool_k), page_t(pool_v))
    o5 = out.reshape(B, N_KV, H_D // N_KV, T, N_KV, HD)
    og = jnp.stack([o5[:, g, :, :, g, :] for g in range(N_KV)], axis=1)
    return og.transpose(0, 3, 1, 2, 4).reshape(B * T, H_D * HD)


def _block_diag(w):
    n, a, b = w.shape
    eye = jnp.eye(n, dtype=w.dtype)
    return (w[:, :, None, :] * eye[:, None, :, None]).reshape(n * a, n * b)


def _prep_weights(P):
    bf = lambda a: a.astype(BF16)
    bw = HB * NB
    pb_w = 3 * bw + W_LORA + A_LORA + G_LORA
    perm = np.concatenate([np.arange(0, bw), np.arange(bw + W_LORA, 3 * bw + W_LORA), np.arange(bw, bw + W_LORA),
                           np.arange(3 * bw + W_LORA, pb_w)])
    W = {"perm": perm, "inv_perm": np.argsort(perm)}
    aw2 = P["e_w_in"].shape[-1] - pb_w
    W["e_w_in"] = [bf(jnp.concatenate([w[:, :aw2], w[:, aw2:][:, perm]], axis=1)) for w in P["e_w_in"]]
    W["b_mu"] = [m[perm] for m in P["b_mu"]]
    W["wB_pad"] = [bf(jnp.concatenate([w, jnp.zeros((A_LORA, bw), F32)], 0)) for w in P["b_wB"]]
    W["aB_pad"] = [bf(jnp.concatenate([jnp.zeros((W_LORA, bw), F32), w], 0)) for w in P["b_aB"]]
    W["gB"] = [bf(w) for w in P["b_gB"]]
    W["ones_bd"] = bf(_block_diag(jnp.ones((HB, NB, NB), F32)))
    o_in = []
    for w in P["o_w_in"]:
        n_main = w.shape[1] - (D_I + H_I)
        o_in.append(bf(jnp.concatenate([w, jnp.zeros((w.shape[0], LANES - (D_I + H_I)), F32)], axis=1)))
        assert n_main % LANES == 0
    W["o_w_in"] = o_in
    W["wx_bd"] = [bf(_block_diag(w)) for w in P["c_wx"]]
    W["wa_bd"] = [bf(_block_diag(w)) for w in P["c_wa"]]
    for name in ("e_w_out", "o_w_out", "w_xq", "w_xo", "w_ff_gate", "w_ff_up", "w_ff_down"):
        W[name] = [bf(w) for w in P[name]]
    W["w_xkv"] = [bf(jnp.concatenate([k, v], axis=1)) for k, v in zip(P["w_xk"], P["w_xv"])]
    return W


def _memory_kv(mem, norm_mem, W):
    B, M, D = mem.shape
    ks, vs = [], []
    for layer in range(len(W["w_xkv"])):
        k, v = rms_matmul(mem.reshape(B * M, D), norm_mem[layer], W["w_xkv"][layer], (D, D))
        ks.append(k.reshape(B, M, D))
        vs.append(v.reshape(B, M, D))
    return ks, vs


def _trunk(x3, pos0, rwkv_s0, rwkv_shift0, lru_h0, lru_conv0, mem_k, mem_v, P, W, sample_ctx):
    B, S, D = x3.shape
    T = B * S
    prompt = sample_ctx is None
    depth = P["norm_mix"].shape[0]
    x = x3.reshape(T, D)
    pos = pos0 + jnp.arange(S, dtype=I32)
    out = {k: [] for k in ("chunk_v", "rwkv_s", "rwkv_sh", "lru_h", "lru_cv", "att_k", "att_v", "att_ki")}
    bw = HB * NB
    for layer in range(depth):
        j = layer // 2
        if layer % 2 == 0:
            aw2 = W["e_w_in"][j].shape[1] - (3 * bw + W_LORA + A_LORA + G_LORA)
            pa, pb = rms_matmul(x, P["norm_mix"][layer], W["e_w_in"][j], (aw2, W["e_w_in"][j].shape[1] - aw2))
            tril = jnp.tril(P["a_ws"][j])
            if S % CHUNK == 0:
                mix, bias = tril, P["a_bs"][j]
            else:
                assert S <= CHUNK
                eye = jnp.eye(B, dtype=F32)
                mix = (tril[:, None, :S, None, :S] * eye[None, :, None, :, None]).reshape(A_G, T, T)
                bias = jnp.tile(P["a_bs"][j][:, :S], (1, B))
            ac = aw2 // 2 // A_G
            bias_full = jnp.broadcast_to(bias[:, :, None], bias.shape + (ac,))
            mixed = chunk_mixer(pa, P["a_ln_g"][j], P["a_ln_b"][j], mix.astype(BF16), bias_full, emit_v=not prompt)
            ya = mixed[0]
            pbw = pb.shape[1]
            sh0 = rwkv_shift0[j][:, W["perm"]]
            r, dec, k, a, v, g, bv, last = rwkv_prep(
                pb.reshape(B, S, pbw), sh0, W["b_mu"][j], P["b_w0"][j], W["wB_pad"][j], P["b_a0"][j], W["aB_pad"][j],
                W["gB"][j], P["b_ka"][j], P["b_rk"][j].reshape(-1), W["ones_bd"])
            Bg = LANES // (2 * HB)
            G = B // Bg
            grp = lambda t: t.reshape(G, Bg, S, bw)
            y_s, s_last = rwkv_scan(_to_scan(grp(r)), _to_scan(grp(dec)), _to_scan(grp(k)), _to_scan(grp(a)),
                                    _v_to_scan(grp(v)), _state_to_scan(rwkv_s0[j].astype(F32).reshape(G, Bg, HB, NB, NB)),
                                    _param_to_scan(P["b_kk"][j], Bg), _param_to_scan(P["b_ka"][j], Bg))
            y = _y_from_scan(y_s, Bg).reshape(T, bw)
            x = mix0_out(ya, y, bv.reshape(T, bw), g.reshape(T, bw), x, P["b_gn_g"][j], P["b_gn_b"][j], W["ones_bd"],
                         W["e_w_out"][j])
            out["chunk_v"].append(mixed[-1].reshape(B, S, -1))
            out["rwkv_s"].append(_state_from_scan(s_last, Bg).reshape(B, HB, NB, NB))
            out["rwkv_sh"].append(last[:, 0][:, W["inv_perm"]])
        else:
            cw2 = 2 * P["c_conv_b"].shape[-1]
            nq, nk, ni = H_D * HD, N_KV * HD, H_I * D_I
            pc, q, k, v, qi, kiw = rms_matmul(x, P["norm_mix"][layer], W["o_w_in"][j], (cw2, nq, nk, nk, ni, LANES))
            yc, h_last, conv_new = lru_mixer(pc.reshape(B, S, cw2), lru_h0[j], lru_conv0[j], P["c_conv_w"][j],
                                             P["c_conv_b"][j], W["wx_bd"][j], P["c_bx"][j], W["wa_bd"][j], P["c_ba"][j],
                                             P["c_lambda"][j], reset_first=prompt)
            tabs_q, tabs_i = _rope_tables(pos, HD), _rope_tables(pos, D_I)
            if not prompt:
                tabs_q, tabs_i = (tuple(jnp.tile(t, (B, 1)) for t in tabs) for tabs in (tabs_q, tabs_i))
            if prompt:
                kr, kiwr, wq, wi, wrow, kbf, kib, vt = dsa_prep(q, k, v, qi, kiw, tabs_q, tabs_i, True)
                yd = dsa_prompt(wq, wi, wrow, kbf, kib, vt, B, S).reshape(T, nq)
            else:
                qr, kr, qir, kiwr = dsa_prep(q, k, v, qi, kiw, tabs_q, tabs_i, False)
                pool_k, pool_v, pool_ki, page_table = sample_ctx
                yd = dsa_sample(qr, kr, v, qir, kiwr, pool_k[j], pool_v[j], pool_ki[j], page_table, B, S)
            x = mix1_out(yc.reshape(T, -1), yd, x, W["o_w_out"][j])
            out["lru_h"].append(h_last)
            out["lru_cv"].append(conv_new)
            out["att_k"].append(kr.reshape(B, S, N_KV, HD))
            out["att_v"].append(v.reshape(B, S, N_KV, HD))
            out["att_ki"].append(kiwr[:, :D_I].reshape(B, S, D_I))
        M = mem_k[layer].shape[1]
        x = cross_attn(x.reshape(B, S, D), P["norm_x"][layer], W["w_xq"][layer], mem_k[layer].reshape(B, M, D),
                       mem_v[layer].reshape(B, M, D), W["w_xo"][layer]).reshape(T, D)
        x = ffn(x, P["norm_ffn"][layer], W["w_ff_gate"][layer], W["w_ff_up"][layer], W["w_ff_down"][layer],
                P["norm_final"], final_norm=(layer == depth - 1))
    st = lambda l: jnp.stack(l, 0)
    return (x.reshape(B, S, D),) + tuple(st(out[k]) for k in
                                         ("chunk_v", "rwkv_s", "rwkv_sh", "lru_h", "lru_cv", "att_k", "att_v", "att_ki"))


def kernel(x_prompt, x_sample, mem_prompt, state_rwkv, state_rwkv_shift, state_lru_h, state_lru_conv, cache_attn_k, cache_attn_v, cache_attn_kidx, cache_mem_k, cache_mem_v, page_table, norm_mix, norm_x, norm_mem, norm_ffn, norm_final, w_xq, w_xk, w_xv, w_xo, w_ff_gate, w_ff_up, w_ff_down, e_w_in, e_w_out, a_ln_g, a_ln_b, a_ws, a_bs, b_mu, b_w0, b_wB, b_a0, b_aB, b_gB, b_kk, b_ka, b_rk, b_gn_g, b_gn_b, o_w_in, o_w_out, c_conv_w, c_conv_b, c_wx, c_bx, c_wa, c_ba, c_lambda):
    P = dict(norm_mix=norm_mix, norm_x=norm_x, norm_ffn=norm_ffn, norm_final=norm_final,
             w_xq=w_xq, w_xk=w_xk, w_xv=w_xv, w_xo=w_xo, w_ff_gate=w_ff_gate, w_ff_up=w_ff_up, w_ff_down=w_ff_down,
             e_w_in=e_w_in, e_w_out=e_w_out, a_ln_g=a_ln_g, a_ln_b=a_ln_b, a_ws=a_ws, a_bs=a_bs,
             b_mu=b_mu, b_w0=b_w0, b_wB=b_wB, b_a0=b_a0, b_aB=b_aB, b_gB=b_gB, b_kk=b_kk, b_ka=b_ka,
             b_rk=b_rk, b_gn_g=b_gn_g, b_gn_b=b_gn_b,
             o_w_in=o_w_in, o_w_out=o_w_out, c_conv_w=c_conv_w, c_conv_b=c_conv_b,
             c_wx=c_wx, c_bx=c_bx, c_wa=c_wa, c_ba=c_ba, c_lambda=c_lambda)
    W = _prep_weights(P)
    dt = x_prompt.dtype
    bp = x_prompt.shape[0]
    n_even, n_odd = state_rwkv.shape[0], state_lru_h.shape[0]
    pb_w = state_rwkv_shift.shape[-1]
    cw = state_lru_h.shape[-1]

    mem_k, mem_v = _memory_kv(mem_prompt, norm_mem, W)
    (y_prompt, _, p_rwkv_state, p_rwkv_shift, p_lru_h, p_lru_conv, p_attn_k, p_attn_v, p_attn_kidx) = _trunk(
        x_prompt, 0,
        jnp.zeros((n_even, bp, HB, NB, NB), dt), jnp.zeros((n_even, bp, pb_w), dt),
        jnp.zeros((n_odd, bp, cw), dt), jnp.zeros((n_odd, bp, CONV_W - 1, cw), dt),
        mem_k, mem_v, P, W, None)
    mem_out = lambda ms: jnp.stack([m.reshape(m.shape[0], m.shape[1], XH, -1) for m in ms], 0)
    p_mem_k, p_mem_v = mem_out(mem_k), mem_out(mem_v)

    past_len = page_table.shape[1] * PAGE_SIZE
    (y_sample, s_chunk_v, s_rwkv_state, s_rwkv_shift, s_lru_h, s_lru_conv, s_attn_k, s_attn_v, s_attn_kidx) = _trunk(
        x_sample, past_len, state_rwkv, state_rwkv_shift, state_lru_h, state_lru_conv,
        cache_mem_k, cache_mem_v, P, W, (cache_attn_k, cache_attn_v, cache_attn_kidx, page_table))

    return (y_prompt, y_sample,
            p_rwkv_state, p_rwkv_shift, p_lru_h, p_lru_conv, p_attn_k, p_attn_v, p_attn_kidx, p_mem_k, p_mem_v,
            s_chunk_v, s_rwkv_state, s_rwkv_shift, s_lru_h, s_lru_conv, s_attn_k, s_attn_v, s_attn_kidx)
```
